```python
import jax, jax.numpy as jnp
from jax import lax
import numpy as np

D_MODEL = 2048
BATCH = 4
SEQ = 2048
DEPTH = 2
DEC_BATCH = 128
DEC_SEQ = 4
PAST_LEN = 16384
PAGE_SIZE = 128

D_LRU = D_MODEL // 2
D_SC = D_MODEL - D_LRU
D_MIX = D_LRU + D_SC
LRU_HEADS = 8
LRU_HEAD_DIM = D_LRU // LRU_HEADS
SC_GROUPS = 8
LRU_CONV_W = 4
SC_CONV_W = 3
RG_LRU_C = 8.0
D_PLE = 256
IN_SPLITS = (D_LRU, D_LRU, D_SC, D_SC, D_SC, D_SC)
IN_WIDTH = sum(IN_SPLITS)
DEEPNORM_ALPHA = (2.0 * DEPTH) ** 0.25
DEEPNORM_BETA = (8.0 * DEPTH) ** -0.25
LN_EPS = 1e-5
GN_EPS = 1e-6

kernel_name = 'hybrid_rglru_shortconv_deepnorm_step'


def causal_dwconv(x, buf, w):
    k_width = w.shape[0]
    t_len = x.shape[1]
    xp = jnp.concatenate([buf.astype(x.dtype), x], axis=1)
    y = sum(xp[:, k:k + t_len] * w[k] for k in range(k_width))
    return y, xp[:, -(k_width - 1):]


def group_rmsnorm(y, g, n_groups):
    b, t, c = y.shape
    yf = y.astype(jnp.float32).reshape(b, t, n_groups, c // n_groups)
    yf = yf * lax.rsqrt(jnp.mean(yf * yf, axis=-1, keepdims=True) + GN_EPS)
    return (yf.reshape(b, t, c) * g.astype(jnp.float32)).astype(y.dtype)


def layer_norm(x, g, b):
    xf = x.astype(jnp.float32)
    mu = jnp.mean(xf, axis=-1, keepdims=True)
    var = jnp.mean(jnp.square(xf - mu), axis=-1, keepdims=True)
    y = (xf - mu) * lax.rsqrt(var + LN_EPS)
    return (y * g.astype(jnp.float32) + b.astype(jnp.float32)).astype(x.dtype)


def rg_lru(xc, w_a, b_a, w_x, b_x, lam, h0):
    b, t, c = xc.shape
    xh = xc.reshape(b, t, LRU_HEADS, LRU_HEAD_DIM)
    r = jax.nn.sigmoid(jnp.einsum('bthi,hij->bthj', xh, w_a) + b_a).reshape(b, t, c)
    i = jax.nn.sigmoid(jnp.einsum('bthi,hij->bthj', xh, w_x) + b_x).reshape(b, t, c)
    log_a = -RG_LRU_C * r.astype(jnp.float32) * jax.nn.softplus(-lam.astype(jnp.float32))
    a = jnp.exp(log_a)
    mult = jnp.sqrt(jnp.maximum(-jnp.expm1(2.0 * log_a), 0.0))
    u = mult * (i * xc).astype(jnp.float32)

    def step(h, au):
        a_t, u_t = au
        h = a_t * h + u_t
        return h, h

    h_last, hs = lax.scan(step, h0.astype(jnp.float32),
                          (jnp.swapaxes(a, 0, 1), jnp.swapaxes(u, 0, 1)))
    return jnp.swapaxes(hs, 0, 1).astype(xc.dtype), h_last.astype(h0.dtype)


def hybrid_layer(x, p, h0, lbuf, sbuf, w_in, lru_conv_w, lru_conv_b, lru_wa, lru_ba,
                 lru_wx, lru_bx, lru_lambda, sc_conv_w, gn_lru, gn_sc, w_out,
                 ple_wp, ple_wg, ple_bg, ln_g, ln_b):
    z = jnp.einsum('btd,de->bte', x, w_in)
    x_l, g_l, b_s, c_s, h_s, g_s = jnp.split(z, np.cumsum(IN_SPLITS)[:-1].tolist(), axis=-1)
    xc, lbuf_new = causal_dwconv(x_l, lbuf, lru_conv_w)
    xc = xc + lru_conv_b
    hs, h_last = rg_lru(xc, lru_wa, lru_ba, lru_wx, lru_bx, lru_lambda, h0)
    y_l = hs * jax.nn.silu(g_l)
    v, sbuf_new = causal_dwconv(c_s * h_s, sbuf, sc_conv_w)
    y_s = b_s * v * jax.nn.silu(g_s)
    y = jnp.concatenate([group_rmsnorm(y_l, gn_lru, LRU_HEADS),
                         group_rmsnorm(y_s, gn_sc, SC_GROUPS)], axis=-1)
    m = jnp.einsum('bte,ed->btd', y, w_out)
    r = DEEPNORM_ALPHA * x + m
    e = jnp.einsum('btk,kd->btd', p, ple_wp)
    gate = jax.nn.sigmoid(jnp.einsum('btd,de->bte', r, ple_wg) + ple_bg)
    x_new = layer_norm(r + gate * e, ln_g, ln_b)
    return x_new, h_last, lbuf_new, sbuf_new


def run_trunk(x, p, h0s, lbufs, sbufs, w_in, lru_conv_w, lru_conv_b, lru_wa, lru_ba,
              lru_wx, lru_bx, lru_lambda, sc_conv_w, gn_lru, gn_sc, w_out,
              ple_wp, ple_wg, ple_bg, ln_g, ln_b):
    hs, lbs, sbs = [], [], []
    for l in range(DEPTH):
        x, h_new, lb_new, sb_new = hybrid_layer(
            x, p[l], h0s[l], lbufs[l], sbufs[l], w_in[l], lru_conv_w[l], lru_conv_b[l],
            lru_wa[l], lru_ba[l], lru_wx[l], lru_bx[l], lru_lambda[l], sc_conv_w[l],
            gn_lru[l], gn_sc[l], w_out[l], ple_wp[l], ple_wg[l], ple_bg[l], ln_g[l], ln_b[l])
        hs.append(h_new)
        lbs.append(lb_new)
        sbs.append(sb_new)
    return x, jnp.stack(hs), jnp.stack(lbs), jnp.stack(sbs)


def setup_inputs(seed: int = 0) -> dict:
    key = jax.random.key(seed)
    ks = jax.random.split(key, 24)
    f32 = jnp.float32

    def nrm(k, shape, scale):
        return jax.random.normal(k, shape, f32) * scale

    a0 = jax.random.uniform(ks[14], (DEPTH, D_LRU), f32, 0.9, 0.999)
    a_base = a0 ** (1.0 / RG_LRU_C)
    return {
        'x_prompt': nrm(ks[0], (BATCH, SEQ, D_MODEL), 1.0),
        'x_sample': nrm(ks[1], (DEC_BATCH, DEC_SEQ, D_MODEL), 1.0),
        'state_lru_h': nrm(ks[2], (DEPTH, DEC_BATCH, D_LRU), 0.5),
        'state_lru_conv': nrm(ks[3], (DEPTH, DEC_BATCH, LRU_CONV_W - 1, D_LRU), 1.0),
        'state_sc_conv': nrm(ks[4], (DEPTH, DEC_BATCH, SC_CONV_W - 1, D_SC), 1.0),
        'p_prompt': nrm(ks[5], (DEPTH, BATCH, SEQ, D_PLE), 1.0),
        'p_sample': nrm(ks[6], (DEPTH, DEC_BATCH, DEC_SEQ, D_PLE), 1.0),
        'w_in': nrm(ks[7], (DEPTH, D_MODEL, IN_WIDTH), D_MODEL ** -0.5),
        'lru_conv_w': nrm(ks[8], (DEPTH, LRU_CONV_W, D_LRU), LRU_CONV_W ** -0.5),
        'lru_conv_b': nrm(ks[9], (DEPTH, D_LRU), 0.01),
        'lru_wa': nrm(ks[10], (DEPTH, LRU_HEADS, LRU_HEAD_DIM, LRU_HEAD_DIM), LRU_HEAD_DIM ** -0.5),
        'lru_ba': nrm(ks[11], (DEPTH, LRU_HEADS, LRU_HEAD_DIM), 0.01),
        'lru_wx': nrm(ks[12], (DEPTH, LRU_HEADS, LRU_HEAD_DIM, LRU_HEAD_DIM), LRU_HEAD_DIM ** -0.5),
        'lru_bx': nrm(ks[13], (DEPTH, LRU_HEADS, LRU_HEAD_DIM), 0.01),
        'lru_lambda': jnp.log(a_base) - jnp.log1p(-a_base),
        'sc_conv_w': nrm(ks[15], (DEPTH, SC_CONV_W, D_SC), SC_CONV_W ** -0.5),
        'gn_lru': 1.0 + nrm(ks[16], (DEPTH, D_LRU), 0.01),
        'gn_sc': 1.0 + nrm(ks[17], (DEPTH, D_SC), 0.01),
        'w_out': nrm(ks[18], (DEPTH, D_MIX, D_MODEL), (D_MIX ** -0.5) * DEEPNORM_BETA),
        'ple_wp': nrm(ks[19], (DEPTH, D_PLE, D_MODEL), D_PLE ** -0.5),
        'ple_wg': nrm(ks[20], (DEPTH, D_MODEL, D_MODEL), D_MODEL ** -0.5),
        'ple_bg': nrm(ks[21], (DEPTH, D_MODEL), 0.01),
        'ln_g': 1.0 + nrm(ks[22], (DEPTH, D_MODEL), 0.01),
        'ln_b': nrm(ks[23], (DEPTH, D_MODEL), 0.01),
    }


def reference(x_prompt, x_sample, state_lru_h, state_lru_conv, state_sc_conv, p_prompt, p_sample,
              w_in, lru_conv_w, lru_conv_b, lru_wa, lru_ba, lru_wx, lru_bx, lru_lambda,
              sc_conv_w, gn_lru, gn_sc, w_out, ple_wp, ple_wg, ple_bg, ln_g, ln_b):
    dt = x_prompt.dtype
    h0_p = jnp.zeros((DEPTH, BATCH, D_LRU), dt)
    lb_p = jnp.zeros((DEPTH, BATCH, LRU_CONV_W - 1, D_LRU), dt)
    sb_p = jnp.zeros((DEPTH, BATCH, SC_CONV_W - 1, D_SC), dt)
    y_prompt, lru_h_prompt, lru_conv_prompt, sc_conv_prompt = run_trunk(
        x_prompt, p_prompt, h0_p, lb_p, sb_p, w_in, lru_conv_w, lru_conv_b, lru_wa, lru_ba,
        lru_wx, lru_bx, lru_lambda, sc_conv_w, gn_lru, gn_sc, w_out, ple_wp, ple_wg, ple_bg,
        ln_g, ln_b)
    y_sample, lru_h_sample, lru_conv_sample, sc_conv_sample = run_trunk(
        x_sample, p_sample, state_lru_h, state_lru_conv, state_sc_conv, w_in, lru_conv_w,
        lru_conv_b, lru_wa, lru_ba, lru_wx, lru_bx, lru_lambda, sc_conv_w, gn_lru, gn_sc,
        w_out, ple_wp, ple_wg, ple_bg, ln_g, ln_b)
    return (y_prompt, y_sample, lru_h_prompt, lru_conv_prompt, sc_conv_prompt,
            lru_h_sample, lru_conv_sample, sc_conv_sample)
```

```python
import functools

import jax
import jax.numpy as jnp
from jax import lax
from jax.experimental import pallas as pl
from jax.experimental.pallas import tpu as pltpu

D_MODEL = 2048
D_LRU = 1024
D_SC = 1024
LRU_HEADS = 8
HEAD_DIM = D_LRU // LRU_HEADS
SC_GROUPS = 8
LRU_CONV_W = 4
SC_CONV_W = 3
RG_LRU_C = 8.0
D_PLE = 256
N_PROJ = 6
LN_EPS = 1e-5
GN_EPS = 1e-6

SUBLANES = 8
PROMPT_TILE = 256
OUT_TILE = 512
VMEM_LIMIT_BYTES = 56 * 1024 * 1024

F32 = jnp.float32
BF16 = jnp.bfloat16


def _sigmoid(v):
    return 1.0 / (1.0 + jnp.exp(-v))


def _silu(v):
    return v * _sigmoid(v)


def _group_rmsnorm(y, gain, n_groups):
    width = y.shape[1] // n_groups
    parts = []
    for g in range(n_groups):
        yg = y[:, g * width:(g + 1) * width]
        ms = jnp.mean(yg * yg, axis=-1, keepdims=True)
        parts.append(yg * lax.rsqrt(ms + GN_EPS))
    return jnp.concatenate(parts, axis=1) * gain


def _lru_gates(xc, wg_ref, ba, bx):
    r_parts, i_parts = [], []
    for h in range(LRU_HEADS):
        xh = xc[:, h * HEAD_DIM:(h + 1) * HEAD_DIM].astype(BF16)
        ri = jnp.dot(xh, wg_ref[h], preferred_element_type=F32)
        r_parts.append(ri[:, :HEAD_DIM])
        i_parts.append(ri[:, HEAD_DIM:])
    r = _sigmoid(jnp.concatenate(r_parts, axis=1) + ba)
    i = _sigmoid(jnp.concatenate(i_parts, axis=1) + bx)
    return r, i


def _lru_coeffs(xc, r, i, lam):
    log_a = r * (-RG_LRU_C * jax.nn.softplus(-lam))
    a = jnp.exp(log_a)
    mult = jnp.sqrt(jnp.maximum(-jnp.tanh(log_a) * (a * a + 1.0), 0.0))
    return a, mult * (i * xc)


def _proj(xb, win_ref, j):
    return jnp.dot(xb, win_ref[:, j * D_LRU:(j + 1) * D_LRU], preferred_element_type=F32)


def _mixer_prompt_kernel(x_ref, win_ref, cw_ref, cb_ref, wg_ref, ba_ref, bx_ref, lam_ref,
                         sw_ref, gnl_ref, gns_ref,
                         y_ref, h_out_ref, lc_out_ref, sc_out_ref,
                         xl_s, ch_s, a_s, u_s, h_s):
    tm = x_ref.shape[0]
    hdr = SUBLANES

    @pl.when(pl.program_id(1) == 0)
    def _():
        xl_s[0:hdr, :] = jnp.zeros((hdr, D_LRU), F32)
        ch_s[0:hdr, :] = jnp.zeros((hdr, D_SC), F32)
        h_s[...] = jnp.zeros_like(h_s)

    xb = x_ref[...].astype(BF16)

    xl_s[hdr:hdr + tm, :] = _proj(xb, win_ref, 0)
    xc = cb_ref[...]
    for k in range(LRU_CONV_W):
        off = hdr - (LRU_CONV_W - 1) + k
        xc = xc + cw_ref[k:k + 1, :] * xl_s[off:off + tm, :]
    tail = xl_s[tm + hdr - (LRU_CONV_W - 1):tm + hdr, :]
    lc_out_ref[...] = tail
    xl_s[hdr - (LRU_CONV_W - 1):hdr, :] = tail

    r, i = _lru_gates(xc, wg_ref, ba_ref[...], bx_ref[...])
    a, u = _lru_coeffs(xc, r, i, lam_ref[...])
    a_s[...] = a
    u_s[...] = u

    def step(t, h):
        h = a_s[pl.ds(t, 1), :] * h + u_s[pl.ds(t, 1), :]
        u_s[pl.ds(t, 1), :] = h
        return h

    h_last = lax.fori_loop(0, tm, step, h_s[...], unroll=8)
    h_s[...] = h_last
    h_out_ref[...] = h_last

    y_l = u_s[...] * _silu(_proj(xb, win_ref, 1))
    y_ref[:, 0:D_LRU] = _group_rmsnorm(y_l, gnl_ref[...], LRU_HEADS).astype(y_ref.dtype)

    b_s = _proj(xb, win_ref, 2)
    ch_s[hdr:hdr + tm, :] = _proj(xb, win_ref, 3) * _proj(xb, win_ref, 4)
    v = None
    for k in range(SC_CONV_W):
        off = hdr - (SC_CONV_W - 1) + k
        term = sw_ref[k:k + 1, :] * ch_s[off:off + tm, :]
        v = term if v is None else v + term
    tail = ch_s[tm + hdr - (SC_CONV_W - 1):tm + hdr, :]
    sc_out_ref[...] = tail
    ch_s[hdr - (SC_CONV_W - 1):hdr, :] = tail

    y_s = b_s * v * _silu(_proj(xb, win_ref, 5))
    y_ref[:, D_LRU:] = _group_rmsnorm(y_s, gns_ref[...], SC_GROUPS).astype(y_ref.dtype)


def _mixer_sample_kernel(x_ref, h0_ref, lb_ref, sb_ref, win_ref, cw_ref, cb_ref, wg_ref,
                         ba_ref, bx_ref, lam_ref, sw_ref, gnl_ref, gns_ref,
                         y_ref, h_out_ref, lc_out_ref, sc_out_ref, *, n_seq, n_t):
    xb = x_ref[...].astype(BF16)

    def slab(v, t):
        return v[t * n_seq:(t + 1) * n_seq, :]

    def causal_conv(buf, cur, w_ref, width):
        hist = [slab(buf, k) for k in range(width - 1)] + [slab(cur, t) for t in range(n_t)]
        outs = []
        for t in range(n_t):
            acc = w_ref[0:1, :] * hist[t]
            for k in range(1, width):
                acc = acc + w_ref[k:k + 1, :] * hist[t + k]
            outs.append(acc)
        new_buf = jnp.concatenate(hist[-(width - 1):], axis=0)
        return jnp.concatenate(outs, axis=0), new_buf

    xl = _proj(xb, win_ref, 0)
    xc, lc_new = causal_conv(lb_ref[...], xl, cw_ref, LRU_CONV_W)
    xc = xc + cb_ref[...]
    lc_out_ref[...] = lc_new

    r, i = _lru_gates(xc, wg_ref, ba_ref[...], bx_ref[...])
    a, u = _lru_coeffs(xc, r, i, lam_ref[...])
    h = h0_ref[...]
    hs = []
    for t in range(n_t):
        h = slab(a, t) * h + slab(u, t)
        hs.append(h)
    h_out_ref[...] = h
    y_l = jnp.concatenate(hs, axis=0) * _silu(_proj(xb, win_ref, 1))
    y_ref[:, 0:D_LRU] = _group_rmsnorm(y_l, gnl_ref[...], LRU_HEADS).astype(y_ref.dtype)

    b_s = _proj(xb, win_ref, 2)
    ch = _proj(xb, win_ref, 3) * _proj(xb, win_ref, 4)
    v, sc_new = causal_conv(sb_ref[...], ch, sw_ref, SC_CONV_W)
    sc_out_ref[...] = sc_new
    y_s = b_s * v * _silu(_proj(xb, win_ref, 5))
    y_ref[:, D_LRU:] = _group_rmsnorm(y_s, gns_ref[...], SC_GROUPS).astype(y_ref.dtype)


def _output_kernel(x_ref, y_ref, p_ref, wout_ref, wp_ref, wgate_ref, bg_ref, lng_ref, lnb_ref,
                   o_ref, *, alpha):
    m = jnp.dot(y_ref[...], wout_ref[...], preferred_element_type=F32)
    r = alpha * x_ref[...] + m
    e = jnp.dot(p_ref[...].astype(BF16), wp_ref[...], preferred_element_type=F32)
    gate = _sigmoid(jnp.dot(r.astype(BF16), wgate_ref[...], preferred_element_type=F32)
                    + bg_ref[...])
    s = r + gate * e
    mu = jnp.mean(s, axis=-1, keepdims=True)
    d = s - mu
    var = jnp.mean(d * d, axis=-1, keepdims=True)
    o_ref[...] = d * lax.rsqrt(var + LN_EPS) * lng_ref[...] + lnb_ref[...]


def _resident(shape, layer):
    nd = len(shape)
    return pl.BlockSpec((None,) + tuple(shape), lambda *_: (layer,) + (0,) * nd,
                        pipeline_mode=pl.Buffered(1))


def _mixer_param_specs(layer):
    return [
        _resident((D_MODEL, N_PROJ * D_LRU), layer),
        _resident((LRU_CONV_W, D_LRU), layer),
        _resident((1, D_LRU), layer),
        _resident((LRU_HEADS, HEAD_DIM, 2 * HEAD_DIM), layer),
        _resident((1, D_LRU), layer),
        _resident((1, D_LRU), layer),
        _resident((1, D_LRU), layer),
        _resident((SC_CONV_W, D_SC), layer),
        _resident((1, D_LRU), layer),
        _resident((1, D_SC), layer),
    ]


def _mixer_prompt(x, params, layer, batch, seq):
    tm = PROMPT_TILE
    nt = seq // tm
    assert seq % tm == 0
    state = lambda k: pl.BlockSpec((None, k, D_LRU), lambda b, t: (b, 0, 0))
    return pl.pallas_call(
        _mixer_prompt_kernel,
        grid=(batch, nt),
        in_specs=[pl.BlockSpec((tm, D_MODEL), lambda b, t: (b * nt + t, 0))]
        + _mixer_param_specs(layer),
        out_specs=[pl.BlockSpec((tm, D_MODEL), lambda b, t: (b * nt + t, 0)),
                   state(1), state(LRU_CONV_W - 1), state(SC_CONV_W - 1)],
        out_shape=[jax.ShapeDtypeStruct((batch * seq, D_MODEL), BF16),
                   jax.ShapeDtypeStruct((batch, 1, D_LRU), F32),
                   jax.ShapeDtypeStruct((batch, LRU_CONV_W - 1, D_LRU), F32),
                   jax.ShapeDtypeStruct((batch, SC_CONV_W - 1, D_SC), F32)],
        scratch_shapes=[pltpu.VMEM((SUBLANES + tm, D_LRU), F32),
                        pltpu.VMEM((SUBLANES + tm, D_SC), F32),
                        pltpu.VMEM((tm, D_LRU), F32),
                        pltpu.VMEM((tm, D_LRU), F32),
                        pltpu.VMEM((1, D_LRU), F32)],
        compiler_params=pltpu.CompilerParams(
            dimension_semantics=("arbitrary", "arbitrary"),
            vmem_limit_bytes=VMEM_LIMIT_BYTES),
        name=f"mixer_prompt_l{layer}",
    )(x, *params)


def _mixer_sample(x_tm, h0, lb_tm, sb_tm, params, layer, n_seq, n_t):
    rows = n_seq * n_t
    whole = lambda shape: pl.BlockSpec(shape, lambda i: (0,) * len(shape))
    per_layer = lambda r: pl.BlockSpec((None, r, D_LRU), lambda i: (layer, 0, 0))
    return pl.pallas_call(
        functools.partial(_mixer_sample_kernel, n_seq=n_seq, n_t=n_t),
        grid=(1,),
        in_specs=[whole((rows, D_MODEL)), per_layer(n_seq),
                  per_layer((LRU_CONV_W - 1) * n_seq), per_layer((SC_CONV_W - 1) * n_seq)]
        + _mixer_param_specs(layer),
        out_specs=[whole((rows, D_MODEL)), whole((n_seq, D_LRU)),
                   whole(((LRU_CONV_W - 1) * n_seq, D_LRU)),
                   whole(((SC_CONV_W - 1) * n_seq, D_SC))],
        out_shape=[jax.ShapeDtypeStruct((rows, D_MODEL), BF16),
                   jax.ShapeDtypeStruct((n_seq, D_LRU), F32),
                   jax.ShapeDtypeStruct(((LRU_CONV_W - 1) * n_seq, D_LRU), F32),
                   jax.ShapeDtypeStruct(((SC_CONV_W - 1) * n_seq, D_SC), F32)],
        compiler_params=pltpu.CompilerParams(
            dimension_semantics=("arbitrary",),
            vmem_limit_bytes=VMEM_LIMIT_BYTES),
        name=f"mixer_sample_l{layer}",
    )(x_tm, h0, lb_tm, sb_tm, *params)


def _output(x, y, p, params, layer, alpha, tag):
    n = x.shape[0]
    tb = min(OUT_TILE, n)
    assert n % tb == 0
    tile = lambda w: pl.BlockSpec((tb, w), lambda i: (i, 0))
    return pl.pallas_call(
        functools.partial(_output_kernel, alpha=alpha),
        grid=(n // tb,),
        in_specs=[tile(D_MODEL), tile(D_MODEL),
                  pl.BlockSpec((None, tb, D_PLE), lambda i: (layer, i, 0)),
                  _resident((D_MODEL, D_MODEL), layer),
                  _resident((D_PLE, D_MODEL), layer),
                  _resident((D_MODEL, D_MODEL), layer),
                  _resident((1, D_MODEL), layer),
                  _resident((1, D_MODEL), layer),
                  _resident((1, D_MODEL), layer)],
        out_specs=tile(D_MODEL),
        out_shape=jax.ShapeDtypeStruct((n, D_MODEL), F32),
        compiler_params=pltpu.CompilerParams(
            dimension_semantics=("arbitrary",),
            vmem_limit_bytes=VMEM_LIMIT_BYTES),
        name=f"output_{tag}_l{layer}",
    )(x, y, p, *params)


def kernel(x_prompt, x_sample, state_lru_h, state_lru_conv, state_sc_conv, p_prompt, p_sample,
           w_in, lru_conv_w, lru_conv_b, lru_wa, lru_ba, lru_wx, lru_bx, lru_lambda,
           sc_conv_w, gn_lru, gn_sc, w_out, ple_wp, ple_wg, ple_bg, ln_g, ln_b):
    depth = w_in.shape[0]
    batch, seq, _ = x_prompt.shape
    n_seq, n_t, _ = x_sample.shape
    alpha = (2.0 * depth) ** 0.25

    row = lambda v: v.reshape(depth, 1, -1)
    mixer_params = (
        w_in.astype(BF16), lru_conv_w, row(lru_conv_b),
        jnp.concatenate([lru_wa, lru_wx], axis=-1).astype(BF16),
        row(lru_ba), row(lru_bx), row(lru_lambda), sc_conv_w, row(gn_lru), row(gn_sc))
    out_params = (w_out.astype(BF16), ple_wp.astype(BF16), ple_wg.astype(BF16),
                  row(ple_bg), row(ln_g), row(ln_b))

    to_tm = lambda v: jnp.swapaxes(v, -3, -2)
    xs = to_tm(x_sample).reshape(n_t * n_seq, D_MODEL)
    ps = to_tm(p_sample).reshape(depth, n_t * n_seq, D_PLE)
    lb_tm = to_tm(state_lru_conv).reshape(depth, (LRU_CONV_W - 1) * n_seq, D_LRU)
    sb_tm = to_tm(state_sc_conv).reshape(depth, (SC_CONV_W - 1) * n_seq, D_SC)

    xp = x_prompt.reshape(batch * seq, D_MODEL)
    pp = p_prompt.reshape(depth, batch * seq, D_PLE)

    hp, lcp, scp, hsm, lcs, scs = [], [], [], [], [], []
    for l in range(depth):
        y, h, lc, sc = _mixer_prompt(xp, mixer_params, l, batch, seq)
        xp = _output(xp, y, pp, out_params, l, alpha, "prompt")
        hp.append(h.reshape(batch, D_LRU)); lcp.append(lc); scp.append(sc)

        y, h, lc, sc = _mixer_sample(xs, state_lru_h, lb_tm, sb_tm, mixer_params, l, n_seq, n_t)
        xs = _output(xs, y, ps, out_params, l, alpha, "sample")
        hsm.append(h)
        lcs.append(to_tm(lc.reshape(LRU_CONV_W - 1, n_seq, D_LRU)))
        scs.append(to_tm(sc.reshape(SC_CONV_W - 1, n_seq, D_SC)))

    y_prompt = xp.reshape(batch, seq, D_MODEL)
    y_sample = to_tm(xs.reshape(n_t, n_seq, D_MODEL))
    return (y_prompt, y_sample, jnp.stack(hp), jnp.stack(lcp), jnp.stack(scp),
            jnp.stack(hsm), jnp.stack(lcs), jnp.stack(scs))
```

```python
import functools

import jax
import jax.numpy as jnp
from jax import lax
from jax.experimental import pallas as pl
from jax.experimental.pallas import tpu as pltpu

D_MODEL = 2048
D_LRU = 1024
D_SC = 1024
LRU_HEADS = 8
HEAD_DIM = D_LRU // LRU_HEADS
SC_GROUPS = 8
LRU_CONV_W = 4
SC_CONV_W = 3
RG_LRU_C = 8.0
D_PLE = 256
N_PROJ = 6
LN_EPS = 1e-5
GN_EPS = 1e-6

SUBLANES = 8
PROMPT_TILE = 256
OUT_TILE = 512
VMEM_LIMIT_BYTES = 56 * 1024 * 1024

F32 = jnp.float32
BF16 = jnp.bfloat16


def _sigmoid(v):
    return 1.0 / (1.0 + jnp.exp(-v))


def _silu(v):
    return v * _sigmoid(v)


def _group_rmsnorm(y, gain, n_groups):
    width = y.shape[1] // n_groups
    parts = []
    for g in range(n_groups):
        yg = y[:, g * width:(g + 1) * width]
        ms = jnp.mean(yg * yg, axis=-1, keepdims=True)
        parts.append(yg * lax.rsqrt(ms + GN_EPS))
    return jnp.concatenate(parts, axis=1) * gain


def _lru_gates(xc, wg_ref, ba, bx):
    r_parts, i_parts = [], []
    for h in range(LRU_HEADS):
        xh = xc[:, h * HEAD_DIM:(h + 1) * HEAD_DIM].astype(BF16)
        ri = jnp.dot(xh, wg_ref[h], preferred_element_type=F32)
        r_parts.append(ri[:, :HEAD_DIM])
        i_parts.append(ri[:, HEAD_DIM:])
    r = _sigmoid(jnp.concatenate(r_parts, axis=1) + ba)
    i = _sigmoid(jnp.concatenate(i_parts, axis=1) + bx)
    return r, i


def _lru_coeffs(xc, r, i, lam):
    log_a = r * (-RG_LRU_C * jax.nn.softplus(-lam))
    a = jnp.exp(log_a)
    mult = jnp.sqrt(jnp.maximum(-jnp.tanh(log_a) * (a * a + 1.0), 0.0))
    return a, mult * (i * xc)


def _proj(xb, win_ref, j):
    return jnp.dot(xb, win_ref[:, j * D_LRU:(j + 1) * D_LRU], preferred_element_type=F32)


def _mixer_prompt_kernel(x_ref, win_ref, cw_ref, cb_ref, wg_ref, ba_ref, bx_ref, lam_ref,
                         sw_ref, gnl_ref, gns_ref,
                         y_ref, h_out_ref, lc_out_ref, sc_out_ref,
                         xl_s, ch_s, a_s, u_s, h_s):
    tm = x_ref.shape[0]
    hdr = SUBLANES

    @pl.when(pl.program_id(1) == 0)
    def _():
        xl_s[0:hdr, :] = jnp.zeros((hdr, D_LRU), F32)
        ch_s[0:hdr, :] = jnp.zeros((hdr, D_SC), F32)
        h_s[...] = jnp.zeros_like(h_s)

    xb = x_ref[...].astype(BF16)

    xl_s[hdr:hdr + tm, :] = _proj(xb, win_ref, 0)
    xc = cb_ref[...]
    for k in range(LRU_CONV_W):
        off = hdr - (LRU_CONV_W - 1) + k
        xc = xc + cw_ref[k:k + 1, :] * xl_s[off:off + tm, :]
    tail = xl_s[tm + hdr - (LRU_CONV_W - 1):tm + hdr, :]
    lc_out_ref[...] = tail
    xl_s[hdr - (LRU_CONV_W - 1):hdr, :] = tail

    r, i = _lru_gates(xc, wg_ref, ba_ref[...], bx_ref[...])
    a, u = _lru_coeffs(xc, r, i, lam_ref[...])
    a_s[...] = a
    u_s[...] = u

    silu_gl = _silu(_proj(xb, win_ref, 1))

    b_s = _proj(xb, win_ref, 2)
    ch_s[hdr:hdr + tm, :] = _proj(xb, win_ref, 3) * _proj(xb, win_ref, 4)
    v = None
    for k in range(SC_CONV_W):
        off = hdr - (SC_CONV_W - 1) + k
        term = sw_ref[k:k + 1, :] * ch_s[off:off + tm, :]
        v = term if v is None else v + term
    tail = ch_s[tm + hdr - (SC_CONV_W - 1):tm + hdr, :]
    sc_out_ref[...] = tail
    ch_s[hdr - (SC_CONV_W - 1):hdr, :] = tail

    y_s = b_s * v * _silu(_proj(xb, win_ref, 5))
    y_ref[:, D_LRU:] = _group_rmsnorm(y_s, gns_ref[...], SC_GROUPS).astype(y_ref.dtype)

    h = h_s[...]
    for t in range(tm):
        h = a_s[t:t + 1, :] * h + u_s[t:t + 1, :]
        u_s[t:t + 1, :] = h
    h_s[...] = h
    h_out_ref[...] = h

    y_l = u_s[...] * silu_gl
    y_ref[:, 0:D_LRU] = _group_rmsnorm(y_l, gnl_ref[...], LRU_HEADS).astype(y_ref.dtype)


def _mixer_sample_kernel(x_ref, h0_ref, lb_ref, sb_ref, win_ref, cw_ref, cb_ref, wg_ref,
                         ba_ref, bx_ref, lam_ref, sw_ref, gnl_ref, gns_ref,
                         y_ref, h_out_ref, lc_out_ref, sc_out_ref, *, n_seq, n_t):
    xb = x_ref[...].astype(BF16)

    def slab(v, t):
        return v[t * n_seq:(t + 1) * n_seq, :]

    def causal_conv(buf, cur, w_ref, width):
        hist = [slab(buf, k) for k in range(width - 1)] + [slab(cur, t) for t in range(n_t)]
        outs = []
        for t in range(n_t):
            acc = w_ref[0:1, :] * hist[t]
            for k in range(1, width):
                acc = acc + w_ref[k:k + 1, :] * hist[t + k]
            outs.append(acc)
        new_buf = jnp.concatenate(hist[-(width - 1):], axis=0)
        return jnp.concatenate(outs, axis=0), new_buf

    xl = _proj(xb, win_ref, 0)
    xc, lc_new = causal_conv(lb_ref[...], xl, cw_ref, LRU_CONV_W)
    xc = xc + cb_ref[...]
    lc_out_ref[...] = lc_new

    r, i = _lru_gates(xc, wg_ref, ba_ref[...], bx_ref[...])
    a, u = _lru_coeffs(xc, r, i, lam_ref[...])
    h = h0_ref[...]
    hs = []
    for t in range(n_t):
        h = slab(a, t) * h + slab(u, t)
        hs.append(h)
    h_out_ref[...] = h
    y_l = jnp.concatenate(hs, axis=0) * _silu(_proj(xb, win_ref, 1))
    y_ref[:, 0:D_LRU] = _group_rmsnorm(y_l, gnl_ref[...], LRU_HEADS).astype(y_ref.dtype)

    b_s = _proj(xb, win_ref, 2)
    ch = _proj(xb, win_ref, 3) * _proj(xb, win_ref, 4)
    v, sc_new = causal_conv(sb_ref[...], ch, sw_ref, SC_CONV_W)
    sc_out_ref[...] = sc_new
    y_s = b_s * v * _silu(_proj(xb, win_ref, 5))
    y_ref[:, D_LRU:] = _group_rmsnorm(y_s, gns_ref[...], SC_GROUPS).astype(y_ref.dtype)


def _output_kernel(x_ref, y_ref, p_ref, wout_ref, wp_ref, wgate_ref, bg_ref, lng_ref, lnb_ref,
                   o_ref, *, alpha):
    m = jnp.dot(y_ref[...], wout_ref[...], preferred_element_type=F32)
    r = alpha * x_ref[...] + m
    e = jnp.dot(p_ref[...].astype(BF16), wp_ref[...], preferred_element_type=F32)
    gate = _sigmoid(jnp.dot(r.astype(BF16), wgate_ref[...], preferred_element_type=F32)
                    + bg_ref[...])
    s = r + gate * e
    mu = jnp.mean(s, axis=-1, keepdims=True)
    d = s - mu
    var = jnp.mean(d * d, axis=-1, keepdims=True)
    o_ref[...] = d * lax.rsqrt(var + LN_EPS) * lng_ref[...] + lnb_ref[...]


def _resident(shape, layer):
    nd = len(shape)
    return pl.BlockSpec((None,) + tuple(shape), lambda *_: (layer,) + (0,) * nd,
                        pipeline_mode=pl.Buffered(1))


def _mixer_param_specs(layer):
    return [
        _resident((D_MODEL, N_PROJ * D_LRU), layer),
        _resident((LRU_CONV_W, D_LRU), layer),
        _resident((1, D_LRU), layer),
        _resident((LRU_HEADS, HEAD_DIM, 2 * HEAD_DIM), layer),
        _resident((1, D_LRU), layer),
        _resident((1, D_LRU), layer),
        _resident((1, D_LRU), layer),
        _resident((SC_CONV_W, D_SC), layer),
        _resident((1, D_LRU), layer),
        _resident((1, D_SC), layer),
    ]


def _mixer_prompt(x, params, layer, batch, seq):
    tm = PROMPT_TILE
    nt = seq // tm
    assert seq % tm == 0
    state = lambda k: pl.BlockSpec((None, k, D_LRU), lambda b, t: (b, 0, 0))
    return pl.pallas_call(
        _mixer_prompt_kernel,
        grid=(batch, nt),
        in_specs=[pl.BlockSpec((tm, D_MODEL), lambda b, t: (b * nt + t, 0))]
        + _mixer_param_specs(layer),
        out_specs=[pl.BlockSpec((tm, D_MODEL), lambda b, t: (b * nt + t, 0)),
                   state(1), state(LRU_CONV_W - 1), state(SC_CONV_W - 1)],
        out_shape=[jax.ShapeDtypeStruct((batch * seq, D_MODEL), BF16),
                   jax.ShapeDtypeStruct((batch, 1, D_LRU), F32),
                   jax.ShapeDtypeStruct((batch, LRU_CONV_W - 1, D_LRU), F32),
                   jax.ShapeDtypeStruct((batch, SC_CONV_W - 1, D_SC), F32)],
        scratch_shapes=[pltpu.VMEM((SUBLANES + tm, D_LRU), F32),
                        pltpu.VMEM((SUBLANES + tm, D_SC), F32),
                        pltpu.VMEM((tm, D_LRU), F32),
                        pltpu.VMEM((tm, D_LRU), F32),
                        pltpu.VMEM((1, D_LRU), F32)],
        compiler_params=pltpu.CompilerParams(
            dimension_semantics=("arbitrary", "arbitrary"),
            vmem_limit_bytes=VMEM_LIMIT_BYTES),
        name=f"mixer_prompt_l{layer}",
    )(x, *params)


def _mixer_sample(x_tm, h0, lb_tm, sb_tm, params, layer, n_seq, n_t):
    rows = n_seq * n_t
    whole = lambda shape: pl.BlockSpec(shape, lambda i: (0,) * len(shape))
    per_layer = lambda r: pl.BlockSpec((None, r, D_LRU), lambda i: (layer, 0, 0))
    return pl.pallas_call(
        functools.partial(_mixer_sample_kernel, n_seq=n_seq, n_t=n_t),
        grid=(1,),
        in_specs=[whole((rows, D_MODEL)), per_layer(n_seq),
                  per_layer((LRU_CONV_W - 1) * n_seq), per_layer((SC_CONV_W - 1) * n_seq)]
        + _mixer_param_specs(layer),
        out_specs=[whole((rows, D_MODEL)), whole((n_seq, D_LRU)),
                   whole(((LRU_CONV_W - 1) * n_seq, D_LRU)),
                   whole(((SC_CONV_W - 1) * n_seq, D_SC))],
        out_shape=[jax.ShapeDtypeStruct((rows, D_MODEL), BF16),
                   jax.ShapeDtypeStruct((n_seq, D_LRU), F32),
                   jax.ShapeDtypeStruct(((LRU_CONV_W - 1) * n_seq, D_LRU), F32),
                   jax.ShapeDtypeStruct(((SC_CONV_W - 1) * n_seq, D_SC), F32)],
        compiler_params=pltpu.CompilerParams(
            dimension_semantics=("arbitrary",),
            vmem_limit_bytes=VMEM_LIMIT_BYTES),
        name=f"mixer_sample_l{layer}",
    )(x_tm, h0, lb_tm, sb_tm, *params)


def _output(x, y, p, params, layer, alpha, tag):
    n = x.shape[0]
    tb = min(OUT_TILE, n)
    assert n % tb == 0
    tile = lambda w: pl.BlockSpec((tb, w), lambda i: (i, 0))
    return pl.pallas_call(
        functools.partial(_output_kernel, alpha=alpha),
        grid=(n // tb,),
        in_specs=[tile(D_MODEL), tile(D_MODEL),
                  pl.BlockSpec((None, tb, D_PLE), lambda i: (layer, i, 0)),
                  _resident((D_MODEL, D_MODEL), layer),
                  _resident((D_PLE, D_MODEL), layer),
                  _resident((D_MODEL, D_MODEL), layer),
                  _resident((1, D_MODEL), layer),
                  _resident((1, D_MODEL), layer),
                  _resident((1, D_MODEL), layer)],
        out_specs=tile(D_MODEL),
        out_shape=jax.ShapeDtypeStruct((n, D_MODEL), F32),
        compiler_params=pltpu.CompilerParams(
            dimension_semantics=("arbitrary",),
            vmem_limit_bytes=VMEM_LIMIT_BYTES),
        name=f"output_{tag}_l{layer}",
    )(x, y, p, *params)


def kernel(x_prompt, x_sample, state_lru_h, state_lru_conv, state_sc_conv, p_prompt, p_sample,
           w_in, lru_conv_w, lru_conv_b, lru_wa, lru_ba, lru_wx, lru_bx, lru_lambda,
           sc_conv_w, gn_lru, gn_sc, w_out, ple_wp, ple_wg, ple_bg, ln_g, ln_b):
    depth = w_in.shape[0]
    batch, seq, _ = x_prompt.shape
    n_seq, n_t, _ = x_sample.shape
    alpha = (2.0 * depth) ** 0.25

    row = lambda v: v.reshape(depth, 1, -1)
    mixer_params = (
        w_in.astype(BF16), lru_conv_w, row(lru_conv_b),
        jnp.concatenate([lru_wa, lru_wx], axis=-1).astype(BF16),
        row(lru_ba), row(lru_bx), row(lru_lambda), sc_conv_w, row(gn_lru), row(gn_sc))
    out_params = (w_out.astype(BF16), ple_wp.astype(BF16), ple_wg.astype(BF16),
                  row(ple_bg), row(ln_g), row(ln_b))

    to_tm = lambda v: jnp.swapaxes(v, -3, -2)
    xs = to_tm(x_sample).reshape(n_t * n_seq, D_MODEL)
    ps = to_tm(p_sample).reshape(depth, n_t * n_seq, D_PLE)
    lb_tm = to_tm(state_lru_conv).reshape(depth, (LRU_CONV_W - 1) * n_seq, D_LRU)
    sb_tm = to_tm(state_sc_conv).reshape(depth, (SC_CONV_W - 1) * n_seq, D_SC)

    xp = x_prompt.reshape(batch * seq, D_MODEL)
    pp = p_prompt.reshape(depth, batch * seq, D_PLE)

    hp, lcp, scp, hsm, lcs, scs = [], [], [], [], [], []
    for l in range(depth):
        y, h, lc, sc = _mixer_prompt(xp, mixer_params, l, batch, seq)
        xp = _output(xp, y, pp, out_params, l, alpha, "prompt")
        hp.append(h.reshape(batch, D_LRU)); lcp.append(lc); scp.append(sc)

        y, h, lc, sc = _mixer_sample(xs, state_lru_h, lb_tm, sb_tm, mixer_params, l, n_seq, n_t)
        xs = _output(xs, y, ps, out_params, l, alpha, "sample")
        hsm.append(h)
        lcs.append(to_tm(lc.reshape(LRU_CONV_W - 1, n_seq, D_LRU)))
        scs.append(to_tm(sc.reshape(SC_CONV_W - 1, n_seq, D_SC)))

    y_prompt = xp.reshape(batch, seq, D_MODEL)
    y_sample = to_tm(xs.reshape(n_t, n_seq, D_MODEL))
    return (y_prompt, y_sample, jnp.stack(hp), jnp.stack(lcp), jnp.stack(scp),
            jnp.stack(hsm), jnp.stack(lcs), jnp.stack(scs))
```

```python
import functools

import jax
import jax.numpy as jnp
from jax import lax
from jax.experimental import pallas as pl
from jax.experimental.pallas import tpu as pltpu

D_MODEL = 2048
D_LRU = 1024
D_SC = 1024
LRU_HEADS = 8
HEAD_DIM = D_LRU // LRU_HEADS
SC_GROUPS = 8
LRU_CONV_W = 4
SC_CONV_W = 3
RG_LRU_C = 8.0
D_PLE = 256
N_PROJ = 6
LN_EPS = 1e-5
GN_EPS = 1e-6

SUBLANES = 8
PROMPT_TILE = 256
MIX_CHUNK = 512
OUT_TILE = 512
VMEM_LIMIT_BYTES = 56 * 1024 * 1024

F32 = jnp.float32
BF16 = jnp.bfloat16


def _sigmoid(v):
    return 1.0 / (1.0 + jnp.exp(-v))


def _silu(v):
    return v * _sigmoid(v)


def _group_rmsnorm(y, gain, n_groups):
    width = y.shape[1] // n_groups
    parts = []
    for g in range(n_groups):
        yg = y[:, g * width:(g + 1) * width]
        ms = jnp.mean(yg * yg, axis=-1, keepdims=True)
        parts.append(yg * lax.rsqrt(ms + GN_EPS))
    return jnp.concatenate(parts, axis=1) * gain


def _lru_gates(xc, wg_ref, heads, ba, bx):
    r_parts, i_parts = [], []
    for n, h in enumerate(heads):
        xh = xc[:, n * HEAD_DIM:(n + 1) * HEAD_DIM].astype(BF16)
        ri = jnp.dot(xh, wg_ref[h], preferred_element_type=F32)
        r_parts.append(ri[:, :HEAD_DIM])
        i_parts.append(ri[:, HEAD_DIM:])
    r = _sigmoid(jnp.concatenate(r_parts, axis=1) + ba)
    i = _sigmoid(jnp.concatenate(i_parts, axis=1) + bx)
    return r, i


def _lru_coeffs(xc, r, i, lam):
    log_a = r * (-RG_LRU_C * jax.nn.softplus(-lam))
    a = jnp.exp(log_a)
    mult = jnp.sqrt(jnp.maximum(-jnp.tanh(log_a) * (a * a + 1.0), 0.0))
    return a, mult * (i * xc)


def _pack_rows(w):
    *lead, k, n = w.shape
    pairs = jnp.swapaxes(w.astype(BF16).reshape(*lead, k // 2, 2, n), -1, -2)
    return lax.bitcast_convert_type(pairs, jnp.uint32)


def _packed_rows(w_ref, cols):
    return pltpu.bitcast(w_ref[:, cols], BF16)


def _proj(xb, win_ref, j):
    return jnp.dot(xb, _packed_rows(win_ref, slice(j * D_LRU, (j + 1) * D_LRU)),
                   preferred_element_type=F32)


def _mixer_prompt_kernel(x_ref, win_ref, cw_ref, cb_ref, wg_ref, ba_ref, bx_ref, lam_ref,
                         sw_ref, gnl_ref, gns_ref,
                         y_ref, h_out_ref, lc_out_ref, sc_out_ref,
                         xl_s, ch_s, a_s, u_s, h_s):
    tm = x_ref.shape[0]
    hdr = SUBLANES

    @pl.when(pl.program_id(1) == 0)
    def _():
        xl_s[0:hdr, :] = jnp.zeros((hdr, D_LRU), F32)
        ch_s[0:hdr, :] = jnp.zeros((hdr, D_SC), F32)
        h_s[...] = jnp.zeros_like(h_s)

    xb = x_ref[...].astype(BF16)

    def proj(j, cols):
        lo = j * D_LRU + cols.start
        return jnp.dot(xb, _packed_rows(win_ref, slice(lo, lo + MIX_CHUNK)),
                       preferred_element_type=F32)

    def causal_conv(cur, hist_s, w_ref, width, tail_ref, cols):
        hist_s[hdr:hdr + tm, cols] = cur
        acc = None
        for k in range(width):
            off = hdr - (width - 1) + k
            term = w_ref[k:k + 1, cols] * hist_s[off:off + tm, cols]
            acc = term if acc is None else acc + term
        tail = hist_s[tm + hdr - (width - 1):tm + hdr, cols]
        tail_ref[:, cols] = tail
        hist_s[hdr - (width - 1):hdr, cols] = tail
        return acc

    for c in range(D_LRU // MIX_CHUNK):
        cols = slice(c * MIX_CHUNK, (c + 1) * MIX_CHUNK)
        heads = range(cols.start // HEAD_DIM, cols.stop // HEAD_DIM)
        sc_cols = slice(D_LRU + cols.start, D_LRU + cols.stop)

        xl = proj(0, cols)
        c_pre, h_pre = proj(3, cols), proj(4, cols)
        xc = causal_conv(xl, xl_s, cw_ref, LRU_CONV_W, lc_out_ref, cols) + cb_ref[:, cols]
        r, i = _lru_gates(xc, wg_ref, heads, ba_ref[:, cols], bx_ref[:, cols])
        b_pre, gs_pre = proj(2, cols), proj(5, cols)
        a, u = _lru_coeffs(xc, r, i, lam_ref[:, cols])
        a_s[:, cols] = a
        u_s[:, cols] = u

        v = causal_conv(c_pre * h_pre, ch_s, sw_ref, SC_CONV_W, sc_out_ref, cols)
        y_s = b_pre * v * _silu(gs_pre)
        y_ref[:, sc_cols] = _group_rmsnorm(y_s, gns_ref[:, cols], len(heads)).astype(y_ref.dtype)

        gl_pre = proj(1, cols)
        h = h_s[:, cols]
        for t in range(tm):
            h = a_s[t:t + 1, cols] * h + u_s[t:t + 1, cols]
            u_s[t:t + 1, cols] = h
        h_s[:, cols] = h
        h_out_ref[:, cols] = h
        y_l = u_s[:, cols] * _silu(gl_pre)
        y_ref[:, cols] = _group_rmsnorm(y_l, gnl_ref[:, cols], len(heads)).astype(y_ref.dtype)


def _mixer_sample_kernel(x_ref, h0_ref, lb_ref, sb_ref, win_ref, cw_ref, cb_ref, wg_ref,
                         ba_ref, bx_ref, lam_ref, sw_ref, gnl_ref, gns_ref,
                         y_ref, h_out_ref, lc_out_ref, sc_out_ref, *, n_seq, n_t):
    xb = x_ref[...].astype(BF16)

    def slab(v, t):
        return v[t * n_seq:(t + 1) * n_seq, :]

    def causal_conv(buf, cur, w_ref, width):
        hist = [slab(buf, k) for k in range(width - 1)] + [slab(cur, t) for t in range(n_t)]
        outs = []
        for t in range(n_t):
            acc = w_ref[0:1, :] * hist[t]
            for k in range(1, width):
                acc = acc + w_ref[k:k + 1, :] * hist[t + k]
            outs.append(acc)
        new_buf = jnp.concatenate(hist[-(width - 1):], axis=0)
        return jnp.concatenate(outs, axis=0), new_buf

    xl = _proj(xb, win_ref, 0)
    xc, lc_new = causal_conv(lb_ref[...], xl, cw_ref, LRU_CONV_W)
    xc = xc + cb_ref[...]
    lc_out_ref[...] = lc_new

    r, i = _lru_gates(xc, wg_ref, range(LRU_HEADS), ba_ref[...], bx_ref[...])
    a, u = _lru_coeffs(xc, r, i, lam_ref[...])
    h = h0_ref[...]
    hs = []
    for t in range(n_t):
        h = slab(a, t) * h + slab(u, t)
        hs.append(h)
    h_out_ref[...] = h
    y_l = jnp.concatenate(hs, axis=0) * _silu(_proj(xb, win_ref, 1))
    y_ref[:, 0:D_LRU] = _group_rmsnorm(y_l, gnl_ref[...], LRU_HEADS).astype(y_ref.dtype)

    b_s = _proj(xb, win_ref, 2)
    ch = _proj(xb, win_ref, 3) * _proj(xb, win_ref, 4)
    v, sc_new = causal_conv(sb_ref[...], ch, sw_ref, SC_CONV_W)
    sc_out_ref[...] = sc_new
    y_s = b_s * v * _silu(_proj(xb, win_ref, 5))
    y_ref[:, D_LRU:] = _group_rmsnorm(y_s, gns_ref[...], SC_GROUPS).astype(y_ref.dtype)


def _output_kernel(x_ref, y_ref, p_ref, wout_ref, wp_ref, wgate_ref, bg_ref, lng_ref, lnb_ref,
                   o_ref, *, alpha):
    m = jnp.dot(y_ref[...], _packed_rows(wout_ref, slice(None)), preferred_element_type=F32)
    r = alpha * x_ref[...] + m
    e = jnp.dot(p_ref[...].astype(BF16), _packed_rows(wp_ref, slice(None)),
                preferred_element_type=F32)
    gate = _sigmoid(jnp.dot(r.astype(BF16), _packed_rows(wgate_ref, slice(None)),
                            preferred_element_type=F32) + bg_ref[...])
    s = r + gate * e
    mu = jnp.mean(s, axis=-1, keepdims=True)
    d = s - mu
    var = jnp.mean(d * d, axis=-1, keepdims=True)
    o_ref[...] = d * lax.rsqrt(var + LN_EPS) * lng_ref[...] + lnb_ref[...]


def _resident(shape, layer):
    nd = len(shape)
    return pl.BlockSpec((None,) + tuple(shape), lambda *_: (layer,) + (0,) * nd,
                        pipeline_mode=pl.Buffered(1))


def _mixer_param_specs(layer):
    return [
        _resident((D_MODEL // 2, N_PROJ * D_LRU), layer),
        _resident((LRU_CONV_W, D_LRU), layer),
        _resident((1, D_LRU), layer),
        _resident((LRU_HEADS, HEAD_DIM, 2 * HEAD_DIM), layer),
        _resident((1, D_LRU), layer),
        _resident((1, D_LRU), layer),
        _resident((1, D_LRU), layer),
        _resident((SC_CONV_W, D_SC), layer),
        _resident((1, D_LRU), layer),
        _resident((1, D_SC), layer),
    ]


def _mixer_prompt(x, params, layer, batch, seq):
    tm = PROMPT_TILE
    nt = seq // tm
    assert seq % tm == 0
    state = lambda k: pl.BlockSpec((None, k, D_LRU), lambda b, t: (b, 0, 0))
    return pl.pallas_call(
        _mixer_prompt_kernel,
        grid=(batch, nt),
        in_specs=[pl.BlockSpec((tm, D_MODEL), lambda b, t: (b * nt + t, 0))]
        + _mixer_param_specs(layer),
        out_specs=[pl.BlockSpec((tm, D_MODEL), lambda b, t: (b * nt + t, 0)),
                   state(1), state(LRU_CONV_W - 1), state(SC_CONV_W - 1)],
        out_shape=[jax.ShapeDtypeStruct((batch * seq, D_MODEL), BF16),
                   jax.ShapeDtypeStruct((batch, 1, D_LRU), F32),
                   jax.ShapeDtypeStruct((batch, LRU_CONV_W - 1, D_LRU), F32),
                   jax.ShapeDtypeStruct((batch, SC_CONV_W - 1, D_SC), F32)],
        scratch_shapes=[pltpu.VMEM((SUBLANES + tm, D_LRU), F32),
                        pltpu.VMEM((SUBLANES + tm, D_SC), F32),
                        pltpu.VMEM((tm, D_LRU), F32),
                        pltpu.VMEM((tm, D_LRU), F32),
                        pltpu.VMEM((1, D_LRU), F32)],
        compiler_params=pltpu.CompilerParams(
            dimension_semantics=("arbitrary", "arbitrary"),
            vmem_limit_bytes=VMEM_LIMIT_BYTES),
        name=f"mixer_prompt_l{layer}",
    )(x, *params)


def _mixer_sample(x_tm, h0, lb_tm, sb_tm, params, layer, n_seq, n_t):
    rows = n_seq * n_t
    whole = lambda shape: pl.BlockSpec(shape, lambda i: (0,) * len(shape))
    per_layer = lambda r: pl.BlockSpec((None, r, D_LRU), lambda i: (layer, 0, 0))
    return pl.pallas_call(
        functools.partial(_mixer_sample_kernel, n_seq=n_seq, n_t=n_t),
        grid=(1,),
        in_specs=[whole((rows, D_MODEL)), per_layer(n_seq),
                  per_layer((LRU_CONV_W - 1) * n_seq), per_layer((SC_CONV_W - 1) * n_seq)]
        + _mixer_param_specs(layer),
        out_specs=[whole((rows, D_MODEL)), whole((n_seq, D_LRU)),
                   whole(((LRU_CONV_W - 1) * n_seq, D_LRU)),
                   whole(((SC_CONV_W - 1) * n_seq, D_SC))],
        out_shape=[jax.ShapeDtypeStruct((rows, D_MODEL), BF16),
                   jax.ShapeDtypeStruct((n_seq, D_LRU), F32),
                   jax.ShapeDtypeStruct(((LRU_CONV_W - 1) * n_seq, D_LRU), F32),
                   jax.ShapeDtypeStruct(((SC_CONV_W - 1) * n_seq, D_SC), F32)],
        compiler_params=pltpu.CompilerParams(
            dimension_semantics=("arbitrary",),
            vmem_limit_bytes=VMEM_LIMIT_BYTES),
        name=f"mixer_sample_l{layer}",
    )(x_tm, h0, lb_tm, sb_tm, *params)


def _output(x, y, p, params, layer, alpha, tag):
    n = x.shape[0]
    tb = min(OUT_TILE, n)
    assert n % tb == 0
    tile = lambda w: pl.BlockSpec((tb, w), lambda i: (i, 0))
    return pl.pallas_call(
        functools.partial(_output_kernel, alpha=alpha),
        grid=(n // tb,),
        in_specs=[tile(D_MODEL), tile(D_MODEL),
                  pl.BlockSpec((None, tb, D_PLE), lambda i: (layer, i, 0)),
                  _resident((D_MODEL // 2, D_MODEL), layer),
                  _resident((D_PLE // 2, D_MODEL), layer),
                  _resident((D_MODEL // 2, D_MODEL), layer),
                  _resident((1, D_MODEL), layer),
                  _resident((1, D_MODEL), layer),
                  _resident((1, D_MODEL), layer)],
        out_specs=tile(D_MODEL),
        out_shape=jax.ShapeDtypeStruct((n, D_MODEL), F32),
        compiler_params=pltpu.CompilerParams(
            dimension_semantics=("arbitrary",),
            vmem_limit_bytes=VMEM_LIMIT_BYTES),
        name=f"output_{tag}_l{layer}",
    )(x, y, p, *params)


def kernel(x_prompt, x_sample, state_lru_h, state_lru_conv, state_sc_conv, p_prompt, p_sample,
           w_in, lru_conv_w, lru_conv_b, lru_wa, lru_ba, lru_wx, lru_bx, lru_lambda,
           sc_conv_w, gn_lru, gn_sc, w_out, ple_wp, ple_wg, ple_bg, ln_g, ln_b):
    depth = w_in.shape[0]
    batch, seq, _ = x_prompt.shape
    n_seq, n_t, _ = x_sample.shape
    alpha = (2.0 * depth) ** 0.25

    row = lambda v: v.reshape(depth, 1, -1)
    mixer_params = (
        _pack_rows(w_in), lru_conv_w, row(lru_conv_b),
        jnp.concatenate([lru_wa, lru_wx], axis=-1).astype(BF16),
        row(lru_ba), row(lru_bx), row(lru_lambda), sc_conv_w, row(gn_lru), row(gn_sc))
    out_params = (_pack_rows(w_out), _pack_rows(ple_wp), _pack_rows(ple_wg),
                  row(ple_bg), row(ln_g), row(ln_b))

    to_tm = lambda v: jnp.swapaxes(v, -3, -2)
    xs = to_tm(x_sample).reshape(n_t * n_seq, D_MODEL)
    ps = to_tm(p_sample).reshape(depth, n_t * n_seq, D_PLE)
    lb_tm = to_tm(state_lru_conv).reshape(depth, (LRU_CONV_W - 1) * n_seq, D_LRU)
    sb_tm = to_tm(state_sc_conv).reshape(depth, (SC_CONV_W - 1) * n_seq, D_SC)

    xp = x_prompt.reshape(batch * seq, D_MODEL)
    pp = p_prompt.reshape(depth, batch * seq, D_PLE)

    hp, lcp, scp, hsm, lcs, scs = [], [], [], [], [], []
    for l in range(depth):
        y, h, lc, sc = _mixer_prompt(xp, mixer_params, l, batch, seq)
        xp = _output(xp, y, pp, out_params, l, alpha, "prompt")
        hp.append(h.reshape(batch, D_LRU)); lcp.append(lc); scp.append(sc)

        y, h, lc, sc = _mixer_sample(xs, state_lru_h, lb_tm, sb_tm, mixer_params, l, n_seq, n_t)
        xs = _output(xs, y, ps, out_params, l, alpha, "sample")
        hsm.append(h)
        lcs.append(to_tm(lc.reshape(LRU_CONV_W - 1, n_seq, D_LRU)))
        scs.append(to_tm(sc.reshape(SC_CONV_W - 1, n_seq, D_SC)))

    y_prompt = xp.reshape(batch, seq, D_MODEL)
    y_sample = to_tm(xs.reshape(n_t, n_seq, D_MODEL))
    return (y_prompt, y_sample, jnp.stack(hp), jnp.stack(lcp), jnp.stack(scp),
            jnp.stack(hsm), jnp.stack(lcs), jnp.stack(scs))
```

```python
import functools

import jax
import jax.numpy as jnp
from jax import lax
from jax.experimental import pallas as pl
from jax.experimental.pallas import tpu as pltpu

D_MODEL = 2048
D_LRU = 1024
D_SC = 1024
LRU_HEADS = 8
HEAD_DIM = D_LRU // LRU_HEADS
SC_GROUPS = 8
LRU_CONV_W = 4
SC_CONV_W = 3
RG_LRU_C = 8.0
D_PLE = 256
N_PROJ = 6
LN_EPS = 1e-5
GN_EPS = 1e-6

SUBLANES = 8
PROMPT_TILE = 256
MIX_CHUNK = 512
OUT_TILE = 512
PACK_ROWS = 256
VMEM_LIMIT_BYTES = 56 * 1024 * 1024

F32 = jnp.float32
BF16 = jnp.bfloat16


def _sigmoid(v):
    return 1.0 / (1.0 + jnp.exp(-v))


def _silu(v):
    return v * _sigmoid(v)


def _group_rmsnorm(y, gain, n_groups):
    width = y.shape[1] // n_groups
    parts = []
    for g in range(n_groups):
        yg = y[:, g * width:(g + 1) * width]
        ms = jnp.mean(yg * yg, axis=-1, keepdims=True)
        parts.append(yg * lax.rsqrt(ms + GN_EPS))
    return jnp.concatenate(parts, axis=1) * gain


def _lru_gates(xc, wg_ref, heads, ba, bx):
    r_parts, i_parts = [], []
    for n, h in enumerate(heads):
        xh = xc[:, n * HEAD_DIM:(n + 1) * HEAD_DIM].astype(BF16)
        ri = jnp.dot(xh, wg_ref[h], preferred_element_type=F32)
        r_parts.append(ri[:, :HEAD_DIM])
        i_parts.append(ri[:, HEAD_DIM:])
    r = _sigmoid(jnp.concatenate(r_parts, axis=1) + ba)
    i = _sigmoid(jnp.concatenate(i_parts, axis=1) + bx)
    return r, i


def _lru_coeffs(xc, r, i, lam):
    log_a = r * (-RG_LRU_C * jax.nn.softplus(-lam))
    a = jnp.exp(log_a)
    mult = jnp.sqrt(jnp.maximum(-jnp.tanh(log_a) * (a * a + 1.0), 0.0))
    return a, mult * (i * xc)


def _pack_rows_kernel(w_ref, o_ref):
    o_ref[...] = pltpu.bitcast(w_ref[...].astype(BF16), jnp.uint32)


def _pack_rows(w):
    depth, k, n = w.shape
    rows = min(k, PACK_ROWS)
    assert k % rows == 0
    return pl.pallas_call(
        _pack_rows_kernel,
        grid=(depth, k // rows),
        in_specs=[pl.BlockSpec((None, rows, n), lambda l, i: (l, i, 0))],
        out_specs=pl.BlockSpec((None, rows // 2, n), lambda l, i: (l, i, 0)),
        out_shape=jax.ShapeDtypeStruct((depth, k // 2, n), jnp.uint32),
        compiler_params=pltpu.CompilerParams(
            dimension_semantics=("arbitrary", "arbitrary"),
            vmem_limit_bytes=VMEM_LIMIT_BYTES),
        name="pack_rows",
    )(w)


def _packed_rows(w_ref, cols):
    return pltpu.bitcast(w_ref[:, cols], BF16)


def _proj(xb, win_ref, j):
    return jnp.dot(xb, _packed_rows(win_ref, slice(j * D_LRU, (j + 1) * D_LRU)),
                   preferred_element_type=F32)


def _mixer_prompt_kernel(x_ref, win_ref, cw_ref, cb_ref, wg_ref, ba_ref, bx_ref, lam_ref,
                         sw_ref, gnl_ref, gns_ref,
                         y_ref, h_out_ref, lc_out_ref, sc_out_ref,
                         xl_s, ch_s, a_s, u_s, h_s):
    tm = x_ref.shape[0]
    hdr = SUBLANES

    @pl.when(pl.program_id(1) == 0)
    def _():
        xl_s[0:hdr, :] = jnp.zeros((hdr, D_LRU), F32)
        ch_s[0:hdr, :] = jnp.zeros((hdr, D_SC), F32)
        h_s[...] = jnp.zeros_like(h_s)

    xb = x_ref[...].astype(BF16)

    def proj(j, cols):
        lo = j * D_LRU + cols.start
        return jnp.dot(xb, _packed_rows(win_ref, slice(lo, lo + MIX_CHUNK)),
                       preferred_element_type=F32)

    def causal_conv(cur, hist_s, w_ref, width, tail_ref, cols):
        hist_s[hdr:hdr + tm, cols] = cur
        acc = None
        for k in range(width):
            off = hdr - (width - 1) + k
            term = w_ref[k:k + 1, cols] * hist_s[off:off + tm, cols]
            acc = term if acc is None else acc + term
        tail = hist_s[tm + hdr - (width - 1):tm + hdr, cols]
        tail_ref[:, cols] = tail
        hist_s[hdr - (width - 1):hdr, cols] = tail
        return acc

    for c in range(D_LRU // MIX_CHUNK):
        cols = slice(c * MIX_CHUNK, (c + 1) * MIX_CHUNK)
        heads = range(cols.start // HEAD_DIM, cols.stop // HEAD_DIM)
        sc_cols = slice(D_LRU + cols.start, D_LRU + cols.stop)

        xl = proj(0, cols)
        c_pre, h_pre = proj(3, cols), proj(4, cols)
        xc = causal_conv(xl, xl_s, cw_ref, LRU_CONV_W, lc_out_ref, cols) + cb_ref[:, cols]
        r, i = _lru_gates(xc, wg_ref, heads, ba_ref[:, cols], bx_ref[:, cols])
        b_pre, gs_pre = proj(2, cols), proj(5, cols)
        a, u = _lru_coeffs(xc, r, i, lam_ref[:, cols])
        a_s[:, cols] = a
        u_s[:, cols] = u

        v = causal_conv(c_pre * h_pre, ch_s, sw_ref, SC_CONV_W, sc_out_ref, cols)
        y_s = b_pre * v * _silu(gs_pre)
        y_ref[:, sc_cols] = _group_rmsnorm(y_s, gns_ref[:, cols], len(heads)).astype(y_ref.dtype)

        gl_pre = proj(1, cols)
        h = h_s[:, cols]
        for t in range(tm):
            h = a_s[t:t + 1, cols] * h + u_s[t:t + 1, cols]
            u_s[t:t + 1, cols] = h
        h_s[:, cols] = h
        h_out_ref[:, cols] = h
        y_l = u_s[:, cols] * _silu(gl_pre)
        y_ref[:, cols] = _group_rmsnorm(y_l, gnl_ref[:, cols], len(heads)).astype(y_ref.dtype)


def _mixer_sample_kernel(x_ref, h0_ref, lb_ref, sb_ref, win_ref, cw_ref, cb_ref, wg_ref,
                         ba_ref, bx_ref, lam_ref, sw_ref, gnl_ref, gns_ref,
                         y_ref, h_out_ref, lc_out_ref, sc_out_ref, *, n_seq, n_t):
    xb = x_ref[...].astype(BF16)

    def slab(v, t):
        return v[t * n_seq:(t + 1) * n_seq, :]

    def causal_conv(buf, cur, w_ref, width):
        hist = [slab(buf, k) for k in range(width - 1)] + [slab(cur, t) for t in range(n_t)]
        outs = []
        for t in range(n_t):
            acc = w_ref[0:1, :] * hist[t]
            for k in range(1, width):
                acc = acc + w_ref[k:k + 1, :] * hist[t + k]
            outs.append(acc)
        new_buf = jnp.concatenate(hist[-(width - 1):], axis=0)
        return jnp.concatenate(outs, axis=0), new_buf

    xl = _proj(xb, win_ref, 0)
    xc, lc_new = causal_conv(lb_ref[...], xl, cw_ref, LRU_CONV_W)
    xc = xc + cb_ref[...]
    lc_out_ref[...] = lc_new

    r, i = _lru_gates(xc, wg_ref, range(LRU_HEADS), ba_ref[...], bx_ref[...])
    a, u = _lru_coeffs(xc, r, i, lam_ref[...])
    h = h0_ref[...]
    hs = []
    for t in range(n_t):
        h = slab(a, t) * h + slab(u, t)
        hs.append(h)
    h_out_ref[...] = h
    y_l = jnp.concatenate(hs, axis=0) * _silu(_proj(xb, win_ref, 1))
    y_ref[:, 0:D_LRU] = _group_rmsnorm(y_l, gnl_ref[...], LRU_HEADS).astype(y_ref.dtype)

    b_s = _proj(xb, win_ref, 2)
    ch = _proj(xb, win_ref, 3) * _proj(xb, win_ref, 4)
    v, sc_new = causal_conv(sb_ref[...], ch, sw_ref, SC_CONV_W)
    sc_out_ref[...] = sc_new
    y_s = b_s * v * _silu(_proj(xb, win_ref, 5))
    y_ref[:, D_LRU:] = _group_rmsnorm(y_s, gns_ref[...], SC_GROUPS).astype(y_ref.dtype)


def _output_kernel(x_ref, y_ref, p_ref, wout_ref, wp_ref, wgate_ref, bg_ref, lng_ref, lnb_ref,
                   o_ref, *, alpha):
    m = jnp.dot(y_ref[...], _packed_rows(wout_ref, slice(None)), preferred_element_type=F32)
    r = alpha * x_ref[...] + m
    e = jnp.dot(p_ref[...].astype(BF16), _packed_rows(wp_ref, slice(None)),
                preferred_element_type=F32)
    gate = _sigmoid(jnp.dot(r.astype(BF16), _packed_rows(wgate_ref, slice(None)),
                            preferred_element_type=F32) + bg_ref[...])
    s = r + gate * e
    mu = jnp.mean(s, axis=-1, keepdims=True)
    d = s - mu
    var = jnp.mean(d * d, axis=-1, keepdims=True)
    o_ref[...] = d * lax.rsqrt(var + LN_EPS) * lng_ref[...] + lnb_ref[...]


def _resident(shape, layer):
    nd = len(shape)
    return pl.BlockSpec((None,) + tuple(shape), lambda *_: (layer,) + (0,) * nd,
                        pipeline_mode=pl.Buffered(1))


def _mixer_param_specs(layer):
    return [
        _resident((D_MODEL // 2, N_PROJ * D_LRU), layer),
        _resident((LRU_CONV_W, D_LRU), layer),
        _resident((1, D_LRU), layer),
        _resident((LRU_HEADS, HEAD_DIM, 2 * HEAD_DIM), layer),
        _resident((1, D_LRU), layer),
        _resident((1, D_LRU), layer),
        _resident((1, D_LRU), layer),
        _resident((SC_CONV_W, D_SC), layer),
        _resident((1, D_LRU), layer),
        _resident((1, D_SC), layer),
    ]


def _mixer_prompt(x, params, layer, batch, seq):
    tm = PROMPT_TILE
    nt = seq // tm
    assert seq % tm == 0
    state = lambda k: pl.BlockSpec((None, k, D_LRU), lambda b, t: (b, 0, 0))
    return pl.pallas_call(
        _mixer_prompt_kernel,
        grid=(batch, nt),
        in_specs=[pl.BlockSpec((tm, D_MODEL), lambda b, t: (b * nt + t, 0))]
        + _mixer_param_specs(layer),
        out_specs=[pl.BlockSpec((tm, D_MODEL), lambda b, t: (b * nt + t, 0)),
                   state(1), state(LRU_CONV_W - 1), state(SC_CONV_W - 1)],
        out_shape=[jax.ShapeDtypeStruct((batch * seq, D_MODEL), BF16),
                   jax.ShapeDtypeStruct((batch, 1, D_LRU), F32),
                   jax.ShapeDtypeStruct((batch, LRU_CONV_W - 1, D_LRU), F32),
                   jax.ShapeDtypeStruct((batch, SC_CONV_W - 1, D_SC), F32)],
        scratch_shapes=[pltpu.VMEM((SUBLANES + tm, D_LRU), F32),
                        pltpu.VMEM((SUBLANES + tm, D_SC), F32),
                        pltpu.VMEM((tm, D_LRU), F32),
                        pltpu.VMEM((tm, D_LRU), F32),
                        pltpu.VMEM((1, D_LRU), F32)],
        compiler_params=pltpu.CompilerParams(
            dimension_semantics=("arbitrary", "arbitrary"),
            vmem_limit_bytes=VMEM_LIMIT_BYTES),
        name=f"mixer_prompt_l{layer}",
    )(x, *params)


def _mixer_sample(x_tm, h0, lb_tm, sb_tm, params, layer, n_seq, n_t):
    rows = n_seq * n_t
    whole = lambda shape: pl.BlockSpec(shape, lambda i: (0,) * len(shape))
    per_layer = lambda r: pl.BlockSpec((None, r, D_LRU), lambda i: (layer, 0, 0))
    return pl.pallas_call(
        functools.partial(_mixer_sample_kernel, n_seq=n_seq, n_t=n_t),
        grid=(1,),
        in_specs=[whole((rows, D_MODEL)), per_layer(n_seq),
                  per_layer((LRU_CONV_W - 1) * n_seq), per_layer((SC_CONV_W - 1) * n_seq)]
        + _mixer_param_specs(layer),
        out_specs=[whole((rows, D_MODEL)), whole((n_seq, D_LRU)),
                   whole(((LRU_CONV_W - 1) * n_seq, D_LRU)),
                   whole(((SC_CONV_W - 1) * n_seq, D_SC))],
        out_shape=[jax.ShapeDtypeStruct((rows, D_MODEL), BF16),
                   jax.ShapeDtypeStruct((n_seq, D_LRU), F32),
                   jax.ShapeDtypeStruct(((LRU_CONV_W - 1) * n_seq, D_LRU), F32),
                   jax.ShapeDtypeStruct(((SC_CONV_W - 1) * n_seq, D_SC), F32)],
        compiler_params=pltpu.CompilerParams(
            dimension_semantics=("arbitrary",),
            vmem_limit_bytes=VMEM_LIMIT_BYTES),
        name=f"mixer_sample_l{layer}",
    )(x_tm, h0, lb_tm, sb_tm, *params)


def _output(x, y, p, params, layer, alpha, tag):
    n = x.shape[0]
    tb = min(OUT_TILE, n)
    assert n % tb == 0
    tile = lambda w: pl.BlockSpec((tb, w), lambda i: (i, 0))
    return pl.pallas_call(
        functools.partial(_output_kernel, alpha=alpha),
        grid=(n // tb,),
        in_specs=[tile(D_MODEL), tile(D_MODEL),
                  pl.BlockSpec((None, tb, D_PLE), lambda i: (layer, i, 0)),
                  _resident((D_MODEL // 2, D_MODEL), layer),
                  _resident((D_PLE // 2, D_MODEL), layer),
                  _resident((D_MODEL // 2, D_MODEL), layer),
                  _resident((1, D_MODEL), layer),
                  _resident((1, D_MODEL), layer),
                  _resident((1, D_MODEL), layer)],
        out_specs=tile(D_MODEL),
        out_shape=jax.ShapeDtypeStruct((n, D_MODEL), F32),
        compiler_params=pltpu.CompilerParams(
            dimension_semantics=("arbitrary",),
            vmem_limit_bytes=VMEM_LIMIT_BYTES),
        name=f"output_{tag}_l{layer}",
    )(x, y, p, *params)


def kernel(x_prompt, x_sample, state_lru_h, state_lru_conv, state_sc_conv, p_prompt, p_sample,
           w_in, lru_conv_w, lru_conv_b, lru_wa, lru_ba, lru_wx, lru_bx, lru_lambda,
           sc_conv_w, gn_lru, gn_sc, w_out, ple_wp, ple_wg, ple_bg, ln_g, ln_b):
    depth = w_in.shape[0]
    batch, seq, _ = x_prompt.shape
    n_seq, n_t, _ = x_sample.shape
    alpha = (2.0 * depth) ** 0.25

    row = lambda v: v.reshape(depth, 1, -1)
    mixer_params = (
        _pack_rows(w_in), lru_conv_w, row(lru_conv_b),
        jnp.concatenate([lru_wa, lru_wx], axis=-1).astype(BF16),
        row(lru_ba), row(lru_bx), row(lru_lambda), sc_conv_w, row(gn_lru), row(gn_sc))
    out_params = (_pack_rows(w_out), _pack_rows(ple_wp), _pack_rows(ple_wg),
                  row(ple_bg), row(ln_g), row(ln_b))

    to_tm = lambda v: jnp.swapaxes(v, -3, -2)
    xs = to_tm(x_sample).reshape(n_t * n_seq, D_MODEL)
    ps = to_tm(p_sample).reshape(depth, n_t * n_seq, D_PLE)
    lb_tm = to_tm(state_lru_conv).reshape(depth, (LRU_CONV_W - 1) * n_seq, D_LRU)
    sb_tm = to_tm(state_sc_conv).reshape(depth, (SC_CONV_W - 1) * n_seq, D_SC)

    xp = x_prompt.reshape(batch * seq, D_MODEL)
    pp = p_prompt.reshape(depth, batch * seq, D_PLE)

    hp, lcp, scp, hsm, lcs, scs = [], [], [], [], [], []
    for l in range(depth):
        y, h, lc, sc = _mixer_prompt(xp, mixer_params, l, batch, seq)
        xp = _output(xp, y, pp, out_params, l, alpha, "prompt")
        hp.append(h.reshape(batch, D_LRU)); lcp.append(lc); scp.append(sc)

        y, h, lc, sc = _mixer_sample(xs, state_lru_h, lb_tm, sb_tm, mixer_params, l, n_seq, n_t)
        xs = _output(xs, y, ps, out_params, l, alpha, "sample")
        hsm.append(h)
        lcs.append(to_tm(lc.reshape(LRU_CONV_W - 1, n_seq, D_LRU)))
        scs.append(to_tm(sc.reshape(SC_CONV_W - 1, n_seq, D_SC)))

    y_prompt = xp.reshape(batch, seq, D_MODEL)
    y_sample = to_tm(xs.reshape(n_t, n_seq, D_MODEL))
    return (y_prompt, y_sample, jnp.stack(hp), jnp.stack(lcp), jnp.stack(scp),
            jnp.stack(hsm), jnp.stack(lcs), jnp.stack(scs))
```

```python
import functools

import jax
import jax.numpy as jnp
from jax import lax
from jax.experimental import pallas as pl
from jax.experimental.pallas import tpu as pltpu

D_MODEL = 2048
D_LRU = 1024
D_SC = 1024
LRU_HEADS = 8
HEAD_DIM = D_LRU // LRU_HEADS
SC_GROUPS = 8
LRU_CONV_W = 4
SC_CONV_W = 3
RG_LRU_C = 8.0
D_PLE = 256
N_PROJ = 6
LN_EPS = 1e-5
GN_EPS = 1e-6

SUBLANES = 8
PROMPT_TILE = 256
MIX_CHUNK = 512
OUT_TILE = 512
PACK_ROWS = 256
VMEM_LIMIT_BYTES = 56 * 1024 * 1024

F32 = jnp.float32
BF16 = jnp.bfloat16


def _sigmoid(v):
    return 1.0 / (1.0 + jnp.exp(-v))


def _silu(v):
    return v * _sigmoid(v)


def _group_rmsnorm(y, gain, n_groups):
    width = y.shape[1] // n_groups
    parts = []
    for g in range(n_groups):
        yg = y[:, g * width:(g + 1) * width]
        ms = jnp.mean(yg * yg, axis=-1, keepdims=True)
        parts.append(yg * lax.rsqrt(ms + GN_EPS))
    return jnp.concatenate(parts, axis=1) * gain


def _lru_gates(xc, wg_ref, heads, ba, bx):
    r_parts, i_parts = [], []
    for n, h in enumerate(heads):
        xh = xc[:, n * HEAD_DIM:(n + 1) * HEAD_DIM].astype(BF16)
        ri = jnp.dot(xh, wg_ref[h], preferred_element_type=F32)
        r_parts.append(ri[:, :HEAD_DIM])
        i_parts.append(ri[:, HEAD_DIM:])
    r = _sigmoid(jnp.concatenate(r_parts, axis=1) + ba)
    i = _sigmoid(jnp.concatenate(i_parts, axis=1) + bx)
    return r, i


def _lru_coeffs(xc, r, i, lam):
    log_a = r * (-RG_LRU_C * jax.nn.softplus(-lam))
    a = jnp.exp(log_a)
    mult = jnp.sqrt(jnp.maximum(-jnp.tanh(log_a) * (a * a + 1.0), 0.0))
    return a, mult * (i * xc)


def _pack_rows_kernel(w_ref, o_ref):
    o_ref[...] = pltpu.bitcast(w_ref[...].astype(BF16), jnp.uint32)


def _pack_rows(w):
    depth, k, n = w.shape
    rows = min(k, PACK_ROWS)
    assert k % rows == 0
    return pl.pallas_call(
        _pack_rows_kernel,
        grid=(depth, k // rows),
        in_specs=[pl.BlockSpec((None, rows, n), lambda l, i: (l, i, 0))],
        out_specs=pl.BlockSpec((None, rows // 2, n), lambda l, i: (l, i, 0)),
        out_shape=jax.ShapeDtypeStruct((depth, k // 2, n), jnp.uint32),
        compiler_params=pltpu.CompilerParams(
            dimension_semantics=("arbitrary", "arbitrary"),
            vmem_limit_bytes=VMEM_LIMIT_BYTES),
        name="pack_rows",
    )(w)


def _packed_rows(w_ref, cols):
    return pltpu.bitcast(w_ref[:, cols], BF16)


def _proj(xb, win_ref, j):
    return jnp.dot(xb, _packed_rows(win_ref, slice(j * D_LRU, (j + 1) * D_LRU)),
                   preferred_element_type=F32)


def _mixer_prompt_kernel(x_ref, win_ref, cw_ref, cb_ref, wg_ref, ba_ref, bx_ref, lam_ref,
                         sw_ref, gnl_ref, gns_ref,
                         y_ref, h_out_ref, lc_out_ref, sc_out_ref,
                         xl_s, ch_s, a_s, u_s, h_s):
    tm = x_ref.shape[0]
    hdr = SUBLANES

    @pl.when(pl.program_id(1) == 0)
    def _():
        xl_s[0:hdr, :] = jnp.zeros((hdr, D_LRU), F32)
        ch_s[0:hdr, :] = jnp.zeros((hdr, D_SC), F32)
        h_s[...] = jnp.zeros_like(h_s)

    xb = x_ref[...].astype(BF16)

    def proj(j, cols):
        lo = j * D_LRU + cols.start
        return jnp.dot(xb, _packed_rows(win_ref, slice(lo, lo + MIX_CHUNK)),
                       preferred_element_type=F32)

    def causal_conv(cur, hist_s, w_ref, width, tail_ref, cols):
        hist_s[hdr:hdr + tm, cols] = cur
        ext = hist_s[0:hdr + tm, cols]
        acc = w_ref[width - 1:width, cols] * cur
        for d in range(1, width):
            shifted = pltpu.roll(ext, d, 0)[hdr:hdr + tm, :]
            acc = acc + w_ref[width - 1 - d:width - d, cols] * shifted
        tail = hist_s[tm + hdr - (width - 1):tm + hdr, cols]
        tail_ref[:, cols] = tail
        hist_s[hdr - (width - 1):hdr, cols] = tail
        return acc

    for c in range(D_LRU // MIX_CHUNK):
        cols = slice(c * MIX_CHUNK, (c + 1) * MIX_CHUNK)
        heads = range(cols.start // HEAD_DIM, cols.stop // HEAD_DIM)
        sc_cols = slice(D_LRU + cols.start, D_LRU + cols.stop)

        xl = proj(0, cols)
        c_pre, h_pre = proj(3, cols), proj(4, cols)
        xc = causal_conv(xl, xl_s, cw_ref, LRU_CONV_W, lc_out_ref, cols) + cb_ref[:, cols]
        r, i = _lru_gates(xc, wg_ref, heads, ba_ref[:, cols], bx_ref[:, cols])
        b_pre, gs_pre = proj(2, cols), proj(5, cols)
        a, u = _lru_coeffs(xc, r, i, lam_ref[:, cols])
        a_s[:, cols] = a
        u_s[:, cols] = u

        v = causal_conv(c_pre * h_pre, ch_s, sw_ref, SC_CONV_W, sc_out_ref, cols)
        y_s = b_pre * v * _silu(gs_pre)
        y_ref[:, sc_cols] = _group_rmsnorm(y_s, gns_ref[:, cols], len(heads)).astype(y_ref.dtype)

        gl_pre = proj(1, cols)
        h = h_s[:, cols]
        for t in range(tm):
            h = a_s[t:t + 1, cols] * h + u_s[t:t + 1, cols]
            u_s[t:t + 1, cols] = h
        h_s[:, cols] = h
        h_out_ref[:, cols] = h
        y_l = u_s[:, cols] * _silu(gl_pre)
        y_ref[:, cols] = _group_rmsnorm(y_l, gnl_ref[:, cols], len(heads)).astype(y_ref.dtype)


def _mixer_sample_kernel(x_ref, h0_ref, lb_ref, sb_ref, win_ref, cw_ref, cb_ref, wg_ref,
                         ba_ref, bx_ref, lam_ref, sw_ref, gnl_ref, gns_ref,
                         y_ref, h_out_ref, lc_out_ref, sc_out_ref, *, n_seq, n_t):
    xb = x_ref[...].astype(BF16)

    def slab(v, t):
        return v[t * n_seq:(t + 1) * n_seq, :]

    def causal_conv(buf, cur, w_ref, width):
        hist = [slab(buf, k) for k in range(width - 1)] + [slab(cur, t) for t in range(n_t)]
        outs = []
        for t in range(n_t):
            acc = w_ref[0:1, :] * hist[t]
            for k in range(1, width):
                acc = acc + w_ref[k:k + 1, :] * hist[t + k]
            outs.append(acc)
        new_buf = jnp.concatenate(hist[-(width - 1):], axis=0)
        return jnp.concatenate(outs, axis=0), new_buf

    xl = _proj(xb, win_ref, 0)
    xc, lc_new = causal_conv(lb_ref[...], xl, cw_ref, LRU_CONV_W)
    xc = xc + cb_ref[...]
    lc_out_ref[...] = lc_new

    r, i = _lru_gates(xc, wg_ref, range(LRU_HEADS), ba_ref[...], bx_ref[...])
    a, u = _lru_coeffs(xc, r, i, lam_ref[...])
    h = h0_ref[...]
    hs = []
    for t in range(n_t):
        h = slab(a, t) * h + slab(u, t)
        hs.append(h)
    h_out_ref[...] = h
    y_l = jnp.concatenate(hs, axis=0) * _silu(_proj(xb, win_ref, 1))
    y_ref[:, 0:D_LRU] = _group_rmsnorm(y_l, gnl_ref[...], LRU_HEADS).astype(y_ref.dtype)

    b_s = _proj(xb, win_ref, 2)
    ch = _proj(xb, win_ref, 3) * _proj(xb, win_ref, 4)
    v, sc_new = causal_conv(sb_ref[...], ch, sw_ref, SC_CONV_W)
    sc_out_ref[...] = sc_new
    y_s = b_s * v * _silu(_proj(xb, win_ref, 5))
    y_ref[:, D_LRU:] = _group_rmsnorm(y_s, gns_ref[...], SC_GROUPS).astype(y_ref.dtype)


def _output_kernel(x_ref, y_ref, p_ref, wout_ref, wp_ref, wgate_ref, bg_ref, lng_ref, lnb_ref,
                   o_ref, *, alpha):
    tb = x_ref.shape[0]
    halves = [slice(0, tb // 2), slice(tb // 2, tb)]
    ms = [jnp.dot(y_ref[h, :], _packed_rows(wout_ref, slice(None)), preferred_element_type=F32)
          for h in halves]
    es = [jnp.dot(p_ref[h, :].astype(BF16), _packed_rows(wp_ref, slice(None)),
                  preferred_element_type=F32) for h in halves]
    rs = [alpha * x_ref[h, :] + m for h, m in zip(halves, ms)]
    gs = [jnp.dot(r.astype(BF16), _packed_rows(wgate_ref, slice(None)),
                  preferred_element_type=F32) for r in rs]
    for h, r, e, g in zip(halves, rs, es, gs):
        s = r + _sigmoid(g + bg_ref[...]) * e
        mu = jnp.mean(s, axis=-1, keepdims=True)
        d = s - mu
        var = jnp.mean(d * d, axis=-1, keepdims=True)
        o_ref[h, :] = d * lax.rsqrt(var + LN_EPS) * lng_ref[...] + lnb_ref[...]


def _resident(shape, layer):
    nd = len(shape)
    return pl.BlockSpec((None,) + tuple(shape), lambda *_: (layer,) + (0,) * nd,
                        pipeline_mode=pl.Buffered(1))


def _mixer_param_specs(layer):
    return [
        _resident((D_MODEL // 2, N_PROJ * D_LRU), layer),
        _resident((LRU_CONV_W, D_LRU), layer),
        _resident((1, D_LRU), layer),
        _resident((LRU_HEADS, HEAD_DIM, 2 * HEAD_DIM), layer),
        _resident((1, D_LRU), layer),
        _resident((1, D_LRU), layer),
        _resident((1, D_LRU), layer),
        _resident((SC_CONV_W, D_SC), layer),
        _resident((1, D_LRU), layer),
        _resident((1, D_SC), layer),
    ]


def _mixer_prompt(x, params, layer, batch, seq):
    tm = PROMPT_TILE
    nt = seq // tm
    assert seq % tm == 0
    state = lambda k: pl.BlockSpec((None, k, D_LRU), lambda b, t: (b, 0, 0))
    return pl.pallas_call(
        _mixer_prompt_kernel,
        grid=(batch, nt),
        in_specs=[pl.BlockSpec((tm, D_MODEL), lambda b, t: (b * nt + t, 0))]
        + _mixer_param_specs(layer),
        out_specs=[pl.BlockSpec((tm, D_MODEL), lambda b, t: (b * nt + t, 0)),
                   state(1), state(LRU_CONV_W - 1), state(SC_CONV_W - 1)],
        out_shape=[jax.ShapeDtypeStruct((batch * seq, D_MODEL), BF16),
                   jax.ShapeDtypeStruct((batch, 1, D_LRU), F32),
                   jax.ShapeDtypeStruct((batch, LRU_CONV_W - 1, D_LRU), F32),
                   jax.ShapeDtypeStruct((batch, SC_CONV_W - 1, D_SC), F32)],
        scratch_shapes=[pltpu.VMEM((SUBLANES + tm, D_LRU), F32),
                        pltpu.VMEM((SUBLANES + tm, D_SC), F32),
                        pltpu.VMEM((tm, D_LRU), F32),
                        pltpu.VMEM((tm, D_LRU), F32),
                        pltpu.VMEM((1, D_LRU), F32)],
        compiler_params=pltpu.CompilerParams(
            dimension_semantics=("arbitrary", "arbitrary"),
            vmem_limit_bytes=VMEM_LIMIT_BYTES),
        name=f"mixer_prompt_l{layer}",
    )(x, *params)


def _mixer_sample(x_tm, h0, lb_tm, sb_tm, params, layer, n_seq, n_t):
    rows = n_seq * n_t
    whole = lambda shape: pl.BlockSpec(shape, lambda i: (0,) * len(shape))
    per_layer = lambda r: pl.BlockSpec((None, r, D_LRU), lambda i: (layer, 0, 0))
    return pl.pallas_call(
        functools.partial(_mixer_sample_kernel, n_seq=n_seq, n_t=n_t),
        grid=(1,),
        in_specs=[whole((rows, D_MODEL)), per_layer(n_seq),
                  per_layer((LRU_CONV_W - 1) * n_seq), per_layer((SC_CONV_W - 1) * n_seq)]
        + _mixer_param_specs(layer),
        out_specs=[whole((rows, D_MODEL)), whole((n_seq, D_LRU)),
                   whole(((LRU_CONV_W - 1) * n_seq, D_LRU)),
                   whole(((SC_CONV_W - 1) * n_seq, D_SC))],
        out_shape=[jax.ShapeDtypeStruct((rows, D_MODEL), BF16),
                   jax.ShapeDtypeStruct((n_seq, D_LRU), F32),
                   jax.ShapeDtypeStruct(((LRU_CONV_W - 1) * n_seq, D_LRU), F32),
                   jax.ShapeDtypeStruct(((SC_CONV_W - 1) * n_seq, D_SC), F32)],
        compiler_params=pltpu.CompilerParams(
            dimension_semantics=("arbitrary",),
            vmem_limit_bytes=VMEM_LIMIT_BYTES),
        name=f"mixer_sample_l{layer}",
    )(x_tm, h0, lb_tm, sb_tm, *params)


def _output(x, y, p, params, layer, alpha, tag):
    n = x.shape[0]
    tb = min(OUT_TILE, n)
    assert n % tb == 0
    tile = lambda w: pl.BlockSpec((tb, w), lambda i: (i, 0))
    return pl.pallas_call(
        functools.partial(_output_kernel, alpha=alpha),
        grid=(n // tb,),
        in_specs=[tile(D_MODEL), tile(D_MODEL),
                  pl.BlockSpec((None, tb, D_PLE), lambda i: (layer, i, 0)),
                  _resident((D_MODEL // 2, D_MODEL), layer),
                  _resident((D_PLE // 2, D_MODEL), layer),
                  _resident((D_MODEL // 2, D_MODEL), layer),
                  _resident((1, D_MODEL), layer),
                  _resident((1, D_MODEL), layer),
                  _resident((1, D_MODEL), layer)],
        out_specs=tile(D_MODEL),
        out_shape=jax.ShapeDtypeStruct((n, D_MODEL), F32),
        compiler_params=pltpu.CompilerParams(
            dimension_semantics=("arbitrary",),
            vmem_limit_bytes=VMEM_LIMIT_BYTES),
        name=f"output_{tag}_l{layer}",
    )(x, y, p, *params)


def kernel(x_prompt, x_sample, state_lru_h, state_lru_conv, state_sc_conv, p_prompt, p_sample,
           w_in, lru_conv_w, lru_conv_b, lru_wa, lru_ba, lru_wx, lru_bx, lru_lambda,
           sc_conv_w, gn_lru, gn_sc, w_out, ple_wp, ple_wg, ple_bg, ln_g, ln_b):
    depth = w_in.shape[0]
    batch, seq, _ = x_prompt.shape
    n_seq, n_t, _ = x_sample.shape
    alpha = (2.0 * depth) ** 0.25

    row = lambda v: v.reshape(depth, 1, -1)
    mixer_params = (
        _pack_rows(w_in), lru_conv_w, row(lru_conv_b),
        jnp.concatenate([lru_wa, lru_wx], axis=-1).astype(BF16),
        row(lru_ba), row(lru_bx), row(lru_lambda), sc_conv_w, row(gn_lru), row(gn_sc))
    out_params = (_pack_rows(w_out), _pack_rows(ple_wp), _pack_rows(ple_wg),
                  row(ple_bg), row(ln_g), row(ln_b))

    to_tm = lambda v: jnp.swapaxes(v, -3, -2)
    xs = to_tm(x_sample).reshape(n_t * n_seq, D_MODEL)
    ps = to_tm(p_sample).reshape(depth, n_t * n_seq, D_PLE)
    lb_tm = to_tm(state_lru_conv).reshape(depth, (LRU_CONV_W - 1) * n_seq, D_LRU)
    sb_tm = to_tm(state_sc_conv).reshape(depth, (SC_CONV_W - 1) * n_seq, D_SC)

    xp = x_prompt.reshape(batch * seq, D_MODEL)
    pp = p_prompt.reshape(depth, batch * seq, D_PLE)

    hp, lcp, scp, hsm, lcs, scs = [], [], [], [], [], []
    for l in range(depth):
        y, h, lc, sc = _mixer_prompt(xp, mixer_params, l, batch, seq)
        xp = _output(xp, y, pp, out_params, l, alpha, "prompt")
        hp.append(h.reshape(batch, D_LRU)); lcp.append(lc); scp.append(sc)

        y, h, lc, sc = _mixer_sample(xs, state_lru_h, lb_tm, sb_tm, mixer_params, l, n_seq, n_t)
        xs = _output(xs, y, ps, out_params, l, alpha, "sample")
        hsm.append(h)
        lcs.append(to_tm(lc.reshape(LRU_CONV_W - 1, n_seq, D_LRU)))
        scs.append(to_tm(sc.reshape(SC_CONV_W - 1, n_seq, D_SC)))

    y_prompt = xp.reshape(batch, seq, D_MODEL)
    y_sample = to_tm(xs.reshape(n_t, n_seq, D_MODEL))
    return (y_prompt, y_sample, jnp.stack(hp), jnp.stack(lcp), jnp.stack(scp),
            jnp.stack(hsm), jnp.stack(lcs), jnp.stack(scs))
```

```python
import functools

import jax
import jax.numpy as jnp
from jax import lax
from jax.experimental import pallas as pl
from jax.experimental.pallas import tpu as pltpu

D_MODEL = 2048
D_LRU = 1024
D_SC = 1024
LRU_HEADS = 8
HEAD_DIM = D_LRU // LRU_HEADS
SC_GROUPS = 8
LRU_CONV_W = 4
SC_CONV_W = 3
RG_LRU_C = 8.0
D_PLE = 256
N_PROJ = 6
LN_EPS = 1e-5
GN_EPS = 1e-6

SUBLANES = 8
MIX_TILE = 256
MIX_CHUNK = 512
OUT_TILE = 512
MIX_W_ROWS = 128
OUT_W_ROWS = 256
VMEM_LIMIT_BYTES = 60 * 1024 * 1024

F32 = jnp.float32
BF16 = jnp.bfloat16


def _sigmoid(v):
    return 1.0 / (1.0 + jnp.exp(-v))


def _silu(v):
    return v * _sigmoid(v)


def _group_rmsnorm(y, gain, n_groups):
    width = y.shape[1] // n_groups
    parts = []
    for g in range(n_groups):
        yg = y[:, g * width:(g + 1) * width]
        ms = jnp.mean(yg * yg, axis=-1, keepdims=True)
        parts.append(yg * lax.rsqrt(ms + GN_EPS))
    return jnp.concatenate(parts, axis=1) * gain


def _lru_gates(xc, wg_ref, heads, ba, bx):
    r_parts, i_parts = [], []
    for n, h in enumerate(heads):
        xh = xc[:, n * HEAD_DIM:(n + 1) * HEAD_DIM].astype(BF16)
        ri = jnp.dot(xh, wg_ref[h], preferred_element_type=F32)
        r_parts.append(ri[:, :HEAD_DIM])
        i_parts.append(ri[:, HEAD_DIM:])
    r = _sigmoid(jnp.concatenate(r_parts, axis=1) + ba)
    i = _sigmoid(jnp.concatenate(i_parts, axis=1) + bx)
    return r, i


def _lru_coeffs(xc, r, i, lam):
    log_a = r * (-RG_LRU_C * jax.nn.softplus(-lam))
    a = jnp.exp(log_a)
    mult = jnp.sqrt(jnp.maximum(-jnp.tanh(log_a) * (a * a + 1.0), 0.0))
    return a, mult * (i * xc)


def _convert_weight_rows(w_ref, wbuf, step):
    half = w_ref.shape[0] // 2
    start = step * half if isinstance(step, int) else pl.multiple_of(step * half, half)
    wbuf[pl.ds(start, half), :] = pltpu.bitcast(w_ref[...].astype(BF16), jnp.uint32)


def _bf16_operand(wbuf, cols):
    return pltpu.bitcast(wbuf[:, cols], BF16)


def _mixer_tile(x_ref, y_ref, wbuf, cb_ref, wg_ref, ba_ref, bx_ref, lam_ref, gnl_ref, gns_ref,
                conv_lru, conv_sc, stage_scan, run_scan):
    xb = x_ref[...].astype(BF16)

    def proj(j, cols):
        lo = j * D_LRU + cols.start
        return jnp.dot(xb, _bf16_operand(wbuf, slice(lo, lo + MIX_CHUNK)),
                       preferred_element_type=F32)

    for c in range(D_LRU // MIX_CHUNK):
        cols = slice(c * MIX_CHUNK, (c + 1) * MIX_CHUNK)
        heads = range(cols.start // HEAD_DIM, cols.stop // HEAD_DIM)
        sc_cols = slice(D_LRU + cols.start, D_LRU + cols.stop)

        xl = proj(0, cols)
        c_pre, h_pre = proj(3, cols), proj(4, cols)
        xc = conv_lru(xl, cols) + cb_ref[:, cols]
        r, i = _lru_gates(xc, wg_ref, heads, ba_ref[:, cols], bx_ref[:, cols])
        b_pre, gs_pre = proj(2, cols), proj(5, cols)
        a, u = _lru_coeffs(xc, r, i, lam_ref[:, cols])
        stage_scan(a, u, cols)

        v = conv_sc(c_pre * h_pre, cols)
        y_s = b_pre * v * _silu(gs_pre)
        y_ref[:, sc_cols] = _group_rmsnorm(y_s, gns_ref[:, cols], len(heads)).astype(y_ref.dtype)

        gl_pre = proj(1, cols)
        y_l = run_scan(cols) * _silu(gl_pre)
        y_ref[:, cols] = _group_rmsnorm(y_l, gnl_ref[:, cols], len(heads)).astype(y_ref.dtype)


def _mixer_kernel(xp_ref, xs_ref, h0_ref, lb_ref, sb_ref, win_ref, cw_ref, cb_ref, wg_ref,
                  ba_ref, bx_ref, lam_ref, sw_ref, gnl_ref, gns_ref,
                  yp_ref, ys_ref, hp_ref, lcp_ref, scp_ref, hs_ref, lcs_ref, scs_ref,
                  wbuf, xl_s, ch_s, a_s, u_s, h_s, *, n_w, n_p, tiles_per_seq, n_t):
    step = pl.program_id(0)
    tm = xp_ref.shape[0]
    common = (wbuf, cb_ref, wg_ref, ba_ref, bx_ref, lam_ref, gnl_ref, gns_ref)

    @pl.when(step < n_w)
    def _():
        _convert_weight_rows(win_ref, wbuf, step)

    @pl.when(jnp.logical_and(step >= n_w, step < n_w + n_p))
    def _():
        hdr = SUBLANES

        @pl.when(lax.rem(step - n_w, tiles_per_seq) == 0)
        def _():
            xl_s[0:hdr, :] = jnp.zeros((hdr, D_LRU), F32)
            ch_s[0:hdr, :] = jnp.zeros((hdr, D_SC), F32)
            h_s[...] = jnp.zeros_like(h_s)

        def causal_conv(cur, hist_s, w_ref, width, tail_ref, cols):
            hist_s[hdr:hdr + tm, cols] = cur
            acc = None
            for k in range(width):
                off = hdr - (width - 1) + k
                term = w_ref[k:k + 1, cols] * hist_s[off:off + tm, cols]
                acc = term if acc is None else acc + term
            tail = hist_s[tm + hdr - (width - 1):tm + hdr, cols]
            tail_ref[:, cols] = tail
            hist_s[hdr - (width - 1):hdr, cols] = tail
            return acc

        def stage_scan(a, u, cols):
            a_s[:, cols] = a
            u_s[:, cols] = u

        def run_scan(cols):
            h = h_s[:, cols]
            for t in range(tm):
                h = a_s[t:t + 1, cols] * h + u_s[t:t + 1, cols]
                u_s[t:t + 1, cols] = h
            h_s[:, cols] = h
            hp_ref[:, cols] = h
            return u_s[:, cols]

        _mixer_tile(
            xp_ref, yp_ref, *common,
            conv_lru=lambda cur, cols: causal_conv(cur, xl_s, cw_ref, LRU_CONV_W, lcp_ref, cols),
            conv_sc=lambda cur, cols: causal_conv(cur, ch_s, sw_ref, SC_CONV_W, scp_ref, cols),
            stage_scan=stage_scan, run_scan=run_scan)

    @pl.when(step >= n_w + n_p)
    def _():
        n_seq = tm // n_t
        staged = {}

        def causal_conv(cur, hist_s, w_ref, width, buf_ref, tail_ref, cols):
            hist = (width - 1) * n_seq
            hist_s[0:hist, cols] = buf_ref[:, cols]
            hist_s[hist:hist + tm, cols] = cur
            acc = None
            for k in range(width):
                term = w_ref[k:k + 1, cols] * hist_s[k * n_seq:k * n_seq + tm, cols]
                acc = term if acc is None else acc + term
            tail_ref[:, cols] = hist_s[tm:tm + hist, cols]
            return acc

        def stage_scan(a, u, cols):
            staged[cols.start] = (a, u)

        def run_scan(cols):
            a, u = staged.pop(cols.start)
            h = h0_ref[:, cols]
            hs = []
            for t in range(n_t):
                rows = slice(t * n_seq, (t + 1) * n_seq)
                h = a[rows, :] * h + u[rows, :]
                hs.append(h)
            hs_ref[:, cols] = h
            return jnp.concatenate(hs, axis=0)

        _mixer_tile(
            xs_ref, ys_ref, *common,
            conv_lru=lambda cur, cols: causal_conv(cur, xl_s, cw_ref, LRU_CONV_W, lb_ref,
                                                   lcs_ref, cols),
            conv_sc=lambda cur, cols: causal_conv(cur, ch_s, sw_ref, SC_CONV_W, sb_ref,
                                                  scs_ref, cols),
            stage_scan=stage_scan, run_scan=run_scan)


def _output_kernel(x_ref, y_ref, p_ref, wout_ref, wp_ref, wgate_ref, bg_ref, lng_ref, lnb_ref,
                   o_ref, wout_buf, wp_buf, wgate_buf, *, alpha, n_w):
    step = pl.program_id(0)

    @pl.when(step < n_w)
    def _():
        _convert_weight_rows(wout_ref, wout_buf, step)
        _convert_weight_rows(wgate_ref, wgate_buf, step)

    @pl.when(step == 0)
    def _():
        _convert_weight_rows(wp_ref, wp_buf, 0)

    @pl.when(step >= n_w)
    def _():
        tb = x_ref.shape[0]
        everything = slice(None)
        halves = [slice(0, tb // 2), slice(tb // 2, tb)]
        ms = [jnp.dot(y_ref[h, :], _bf16_operand(wout_buf, everything),
                      preferred_element_type=F32) for h in halves]
        es = [jnp.dot(p_ref[h, :].astype(BF16), _bf16_operand(wp_buf, everything),
                      preferred_element_type=F32) for h in halves]
        rs = [alpha * x_ref[h, :] + m for h, m in zip(halves, ms)]
        gs = [jnp.dot(r.astype(BF16), _bf16_operand(wgate_buf, everything),
                      preferred_element_type=F32) for r in rs]
        for h, r, e, g in zip(halves, rs, es, gs):
            s = r + _sigmoid(g + bg_ref[...]) * e
            mu = jnp.mean(s, axis=-1, keepdims=True)
            d = s - mu
            var = jnp.mean(d * d, axis=-1, keepdims=True)
            o_ref[h, :] = d * lax.rsqrt(var + LN_EPS) * lng_ref[...] + lnb_ref[...]


def _resident(shape, layer):
    nd = len(shape)
    return pl.BlockSpec((None,) + tuple(shape), lambda *_: (layer,) + (0,) * nd,
                        pipeline_mode=pl.Buffered(1))


def _weight_rows(rows, n_cols, layer):
    last = D_MODEL // rows - 1
    return pl.BlockSpec((None, rows, n_cols), lambda s: (layer, jnp.minimum(s, last), 0))


def _mixer(xp, xs, h0, lb, sb, params, layer, batch, seq, n_seq, n_t):
    tm = MIX_TILE
    tiles_per_seq = seq // tm
    n_p = batch * tiles_per_seq
    n_s = (n_seq * n_t) // tm
    n_w = D_MODEL // MIX_W_ROWS
    seq_per_tile = tm // n_t
    assert seq % tm == 0 and (n_seq * n_t) % tm == 0 and tm % n_t == 0

    prompt_tile = lambda s: jnp.clip(s - n_w, 0, n_p - 1)
    sample_tile = lambda s: jnp.clip(s - n_w - n_p, 0, n_s - 1)
    p_tok = pl.BlockSpec((tm, D_MODEL), lambda s: (prompt_tile(s), 0))
    s_tok = pl.BlockSpec((tm, D_MODEL), lambda s: (sample_tile(s), 0))
    p_state = lambda k: pl.BlockSpec((None, k, D_LRU),
                                     lambda s: (prompt_tile(s) // tiles_per_seq, 0, 0))
    s_state_in = lambda k: pl.BlockSpec((None, k * seq_per_tile, D_LRU),
                                        lambda s: (layer, sample_tile(s), 0))
    s_state_out = lambda k: pl.BlockSpec((k * seq_per_tile, D_LRU), lambda s: (sample_tile(s), 0))

    (w_in, cw, cb, wg, ba, bx, lam, sw, gnl, gns) = params
    return pl.pallas_call(
        functools.partial(_mixer_kernel, n_w=n_w, n_p=n_p, tiles_per_seq=tiles_per_seq, n_t=n_t),
        grid=(n_w + n_p + n_s,),
        in_specs=[p_tok, s_tok, s_state_in(1), s_state_in(LRU_CONV_W - 1),
                  s_state_in(SC_CONV_W - 1),
                  _weight_rows(MIX_W_ROWS, N_PROJ * D_LRU, layer),
                  _resident((LRU_CONV_W, D_LRU), layer),
                  _resident((1, D_LRU), layer),
                  _resident((LRU_HEADS, HEAD_DIM, 2 * HEAD_DIM), layer),
                  _resident((1, D_LRU), layer),
                  _resident((1, D_LRU), layer),
                  _resident((1, D_LRU), layer),
                  _resident((SC_CONV_W, D_SC), layer),
                  _resident((1, D_LRU), layer),
                  _resident((1, D_SC), layer)],
        out_specs=[p_tok, s_tok, p_state(1), p_state(LRU_CONV_W - 1), p_state(SC_CONV_W - 1),
                   s_state_out(1), s_state_out(LRU_CONV_W - 1), s_state_out(SC_CONV_W - 1)],
        out_shape=[jax.ShapeDtypeStruct((batch * seq, D_MODEL), BF16),
                   jax.ShapeDtypeStruct((n_seq * n_t, D_MODEL), BF16),
                   jax.ShapeDtypeStruct((batch, 1, D_LRU), F32),
                   jax.ShapeDtypeStruct((batch, LRU_CONV_W - 1, D_LRU), F32),
                   jax.ShapeDtypeStruct((batch, SC_CONV_W - 1, D_SC), F32),
                   jax.ShapeDtypeStruct((n_seq, D_LRU), F32),
                   jax.ShapeDtypeStruct(((LRU_CONV_W - 1) * n_seq, D_LRU), F32),
                   jax.ShapeDtypeStruct(((SC_CONV_W - 1) * n_seq, D_SC), F32)],
        scratch_shapes=[pltpu.VMEM((D_MODEL // 2, N_PROJ * D_LRU), jnp.uint32),
                        pltpu.VMEM(((LRU_CONV_W - 1) * seq_per_tile + tm, D_LRU), F32),
                        pltpu.VMEM(((LRU_CONV_W - 1) * seq_per_tile + tm, D_SC), F32),
                        pltpu.VMEM((tm, D_LRU), F32),
                        pltpu.VMEM((tm, D_LRU), F32),
                        pltpu.VMEM((1, D_LRU), F32)],
        compiler_params=pltpu.CompilerParams(
            dimension_semantics=("arbitrary",),
            vmem_limit_bytes=VMEM_LIMIT_BYTES),
        name=f"mixer_l{layer}",
    )(xp, xs, h0, lb, sb, w_in, cw, cb, wg, ba, bx, lam, sw, gnl, gns)


def _output(x, y, p, params, layer, alpha, tag):
    n = x.shape[0]
    tb = min(OUT_TILE, n)
    n_w = D_MODEL // OUT_W_ROWS
    assert n % tb == 0
    tile_of = lambda s: jnp.maximum(s - n_w, 0)
    tile = lambda w: pl.BlockSpec((tb, w), lambda s: (tile_of(s), 0))
    (w_out, wp, wgate, bg, lng, lnb) = params
    return pl.pallas_call(
        functools.partial(_output_kernel, alpha=alpha, n_w=n_w),
        grid=(n_w + n // tb,),
        in_specs=[tile(D_MODEL), tile(D_MODEL),
                  pl.BlockSpec((None, tb, D_PLE), lambda s: (layer, tile_of(s), 0)),
                  _weight_rows(OUT_W_ROWS, D_MODEL, layer),
                  _resident((D_PLE, D_MODEL), layer),
                  _weight_rows(OUT_W_ROWS, D_MODEL, layer),
                  _resident((1, D_MODEL), layer),
                  _resident((1, D_MODEL), layer),
                  _resident((1, D_MODEL), layer)],
        out_specs=tile(D_MODEL),
        out_shape=jax.ShapeDtypeStruct((n, D_MODEL), F32),
        scratch_shapes=[pltpu.VMEM((D_MODEL // 2, D_MODEL), jnp.uint32),
                        pltpu.VMEM((D_PLE // 2, D_MODEL), jnp.uint32),
                        pltpu.VMEM((D_MODEL // 2, D_MODEL), jnp.uint32)],
        compiler_params=pltpu.CompilerParams(
            dimension_semantics=("arbitrary",),
            vmem_limit_bytes=VMEM_LIMIT_BYTES),
        name=f"output_{tag}_l{layer}",
    )(x, y, p, w_out, wp, wgate, bg, lng, lnb)


def kernel(x_prompt, x_sample, state_lru_h, state_lru_conv, state_sc_conv, p_prompt, p_sample,
           w_in, lru_conv_w, lru_conv_b, lru_wa, lru_ba, lru_wx, lru_bx, lru_lambda,
           sc_conv_w, gn_lru, gn_sc, w_out, ple_wp, ple_wg, ple_bg, ln_g, ln_b):
    depth = w_in.shape[0]
    batch, seq, _ = x_prompt.shape
    n_seq, n_t, _ = x_sample.shape
    alpha = (2.0 * depth) ** 0.25
    seq_per_tile = MIX_TILE // n_t
    n_s = n_seq // seq_per_tile

    row = lambda v: v.reshape(depth, 1, -1)
    mixer_params = (
        w_in, lru_conv_w, row(lru_conv_b),
        jnp.concatenate([lru_wa, lru_wx], axis=-1).astype(BF16),
        row(lru_ba), row(lru_bx), row(lru_lambda), sc_conv_w, row(gn_lru), row(gn_sc))
    out_params = (w_out, ple_wp, ple_wg, row(ple_bg), row(ln_g), row(ln_b))

    def to_slabs(v):
        *lead, n, k, c = v.shape
        v = jnp.swapaxes(v.reshape(*lead, n_s, seq_per_tile, k, c), -3, -2)
        return v.reshape(*lead, n * k, c)

    def from_slabs(v, k):
        c = v.shape[-1]
        v = jnp.swapaxes(v.reshape(n_s, k, seq_per_tile, c), 1, 2)
        return v.reshape(n_seq, k, c)

    xs = to_slabs(x_sample)
    ps = to_slabs(p_sample)
    lb = to_slabs(state_lru_conv)
    sb = to_slabs(state_sc_conv)

    xp = x_prompt.reshape(batch * seq, D_MODEL)
    pp = p_prompt.reshape(depth, batch * seq, D_PLE)

    hp, lcp, scp, hsm, lcs, scs = [], [], [], [], [], []
    for l in range(depth):
        yp, ys, h_p, lc_p, sc_p, h_s, lc_s, sc_s = _mixer(
            xp, xs, state_lru_h, lb, sb, mixer_params, l, batch, seq, n_seq, n_t)
        xp = _output(xp, yp, pp, out_params, l, alpha, "prompt")
        xs = _output(xs, ys, ps, out_params, l, alpha, "sample")
        hp.append(h_p.reshape(batch, D_LRU)); lcp.append(lc_p); scp.append(sc_p)
        hsm.append(h_s)
        lcs.append(from_slabs(lc_s, LRU_CONV_W - 1))
        scs.append(from_slabs(sc_s, SC_CONV_W - 1))

    y_prompt = xp.reshape(batch, seq, D_MODEL)
    y_sample = from_slabs(xs, n_t)
    return (y_prompt, y_sample, jnp.stack(hp), jnp.stack(lcp), jnp.stack(scp),
            jnp.stack(hsm), jnp.stack(lcs), jnp.stack(scs))
```

```python
import functools

import jax
import jax.numpy as jnp
from jax import lax
from jax.experimental import pallas as pl
from jax.experimental.pallas import tpu as pltpu

D_MODEL = 2048
D_LRU = 1024
D_SC = 1024
LRU_HEADS = 8
HEAD_DIM = D_LRU // LRU_HEADS
SC_GROUPS = 8
LRU_CONV_W = 4
SC_CONV_W = 3
RG_LRU_C = 8.0
D_PLE = 256
N_PROJ = 6
LN_EPS = 1e-5
GN_EPS = 1e-6

SUBLANES = 8
MIX_TILE = 256
MIX_CHUNK = 512
OUT_TILE = 512
MIX_W_ROWS = 128
OUT_W_ROWS = 256
VMEM_LIMIT_BYTES = 60 * 1024 * 1024

F32 = jnp.float32
BF16 = jnp.bfloat16


def _sigmoid(v):
    return 1.0 / (1.0 + jnp.exp(-v))


def _silu(v):
    return v * _sigmoid(v)


def _group_rmsnorm(y, gain, n_groups):
    width = y.shape[1] // n_groups
    parts = []
    for g in range(n_groups):
        yg = y[:, g * width:(g + 1) * width]
        ms = jnp.mean(yg * yg, axis=-1, keepdims=True)
        parts.append(yg * lax.rsqrt(ms + GN_EPS))
    return jnp.concatenate(parts, axis=1) * gain


def _lru_gates(xc, wg_ref, heads, ba, bx):
    r_parts, i_parts = [], []
    for n, h in enumerate(heads):
        xh = xc[:, n * HEAD_DIM:(n + 1) * HEAD_DIM].astype(BF16)
        ri = jnp.dot(xh, wg_ref[h], preferred_element_type=F32)
        r_parts.append(ri[:, :HEAD_DIM])
        i_parts.append(ri[:, HEAD_DIM:])
    r = _sigmoid(jnp.concatenate(r_parts, axis=1) + ba)
    i = _sigmoid(jnp.concatenate(i_parts, axis=1) + bx)
    return r, i


def _lru_coeffs(xc, r, i, lam):
    log_a = r * (-RG_LRU_C * jax.nn.softplus(-lam))
    a = jnp.exp(log_a)
    mult = jnp.sqrt(jnp.maximum(-jnp.tanh(log_a) * (a * a + 1.0), 0.0))
    return a, mult * (i * xc)


def _convert_weight_rows(w_ref, wbuf, step):
    half = w_ref.shape[0] // 2
    start = step * half if isinstance(step, int) else pl.multiple_of(step * half, half)
    wbuf[pl.ds(start, half), :] = pltpu.bitcast(w_ref[...].astype(BF16), jnp.uint32)


def _bf16_operand(wbuf, cols):
    return pltpu.bitcast(wbuf[:, cols], BF16)


def _mixer_tile(x_ref, y_ref, xb_s, wbuf, cb_ref, wg_ref, ba_ref, bx_ref, lam_ref, gnl_ref,
                gns_ref, conv_lru, conv_sc, stage_scan, run_scan):
    xb_s[...] = x_ref[...].astype(BF16)

    def proj(j, cols):
        lo = j * D_LRU + cols.start
        return jnp.dot(xb_s[...], _bf16_operand(wbuf, slice(lo, lo + MIX_CHUNK)),
                       preferred_element_type=F32)

    for c in range(D_LRU // MIX_CHUNK):
        cols = slice(c * MIX_CHUNK, (c + 1) * MIX_CHUNK)
        heads = range(cols.start // HEAD_DIM, cols.stop // HEAD_DIM)
        sc_cols = slice(D_LRU + cols.start, D_LRU + cols.stop)

        xl = proj(0, cols)
        c_pre, h_pre = proj(3, cols), proj(4, cols)
        xc = conv_lru(xl, cols) + cb_ref[:, cols]
        r, i = _lru_gates(xc, wg_ref, heads, ba_ref[:, cols], bx_ref[:, cols])
        b_pre, gs_pre = proj(2, cols), proj(5, cols)
        a, u = _lru_coeffs(xc, r, i, lam_ref[:, cols])
        stage_scan(a, u, cols)

        v = conv_sc(c_pre * h_pre, cols)
        y_s = b_pre * v * _silu(gs_pre)
        y_ref[:, sc_cols] = _group_rmsnorm(y_s, gns_ref[:, cols], len(heads)).astype(y_ref.dtype)

        gl_pre = proj(1, cols)
        y_l = run_scan(cols) * _silu(gl_pre)
        y_ref[:, cols] = _group_rmsnorm(y_l, gnl_ref[:, cols], len(heads)).astype(y_ref.dtype)


def _mixer_kernel(xp_ref, xs_ref, h0_ref, lb_ref, sb_ref, win_ref, cw_ref, cb_ref, wg_ref,
                  ba_ref, bx_ref, lam_ref, sw_ref, gnl_ref, gns_ref,
                  yp_ref, ys_ref, hp_ref, lcp_ref, scp_ref, hs_ref, lcs_ref, scs_ref,
                  wbuf, xl_s, ch_s, h_s, xb_s, *, n_w, n_p, tiles_per_seq, n_t):
    step = pl.program_id(0)
    tm = xp_ref.shape[0]
    common = (xb_s, wbuf, cb_ref, wg_ref, ba_ref, bx_ref, lam_ref, gnl_ref, gns_ref)

    @pl.when(step < n_w)
    def _():
        _convert_weight_rows(win_ref, wbuf, step)

    @pl.when(jnp.logical_and(step >= n_w, step < n_w + n_p))
    def _():
        hdr = SUBLANES

        @pl.when(lax.rem(step - n_w, tiles_per_seq) == 0)
        def _():
            xl_s[0:hdr, :] = jnp.zeros((hdr, D_LRU), F32)
            ch_s[0:hdr, :] = jnp.zeros((hdr, D_SC), F32)
            h_s[...] = jnp.zeros_like(h_s)

        def causal_conv(cur, hist_s, w_ref, width, tail_ref, cols):
            hist_s[hdr:hdr + tm, cols] = cur
            acc = None
            for k in range(width):
                off = hdr - (width - 1) + k
                term = w_ref[k:k + 1, cols] * hist_s[off:off + tm, cols]
                acc = term if acc is None else acc + term
            tail = hist_s[tm + hdr - (width - 1):tm + hdr, cols]
            tail_ref[:, cols] = tail
            hist_s[hdr - (width - 1):hdr, cols] = tail
            return acc

        staged = {}

        def stage_scan(a, u, cols):
            staged[cols.start] = (a, u)

        def run_scan(cols):
            a, u = staged.pop(cols.start)
            row = lax.broadcasted_iota(jnp.int32, (SUBLANES, a.shape[1]), 0)
            last = slice(SUBLANES - 1, SUBLANES)
            cums = []
            for blk in range(tm // SUBLANES):
                rows = slice(blk * SUBLANES, (blk + 1) * SUBLANES)
                a_cum, u_cum = a[rows, :], u[rows, :]
                d = 1
                while d < SUBLANES:
                    has_prev = row >= d
                    a_prev = jnp.where(has_prev, pltpu.roll(a_cum, d, 0), 1.0)
                    u_prev = jnp.where(has_prev, pltpu.roll(u_cum, d, 0), 0.0)
                    u_cum = a_cum * u_prev + u_cum
                    a_cum = a_cum * a_prev
                    d *= 2
                cums.append((a_cum, u_cum))
            h = h_s[:, cols]
            h_in = []
            for a_cum, u_cum in cums:
                h_in.append(h)
                h = a_cum[last, :] * h + u_cum[last, :]
            h_s[:, cols] = h
            hp_ref[:, cols] = h
            return jnp.concatenate(
                [a_cum * h0 + u_cum for (a_cum, u_cum), h0 in zip(cums, h_in)], axis=0)

        _mixer_tile(
            xp_ref, yp_ref, *common,
            conv_lru=lambda cur, cols: causal_conv(cur, xl_s, cw_ref, LRU_CONV_W, lcp_ref, cols),
            conv_sc=lambda cur, cols: causal_conv(cur, ch_s, sw_ref, SC_CONV_W, scp_ref, cols),
            stage_scan=stage_scan, run_scan=run_scan)

    @pl.when(step >= n_w + n_p)
    def _():
        n_seq = tm // n_t
        staged = {}

        def causal_conv(cur, hist_s, w_ref, width, buf_ref, tail_ref, cols):
            hist = (width - 1) * n_seq
            hist_s[0:hist, cols] = buf_ref[:, cols]
            hist_s[hist:hist + tm, cols] = cur
            acc = None
            for k in range(width):
                term = w_ref[k:k + 1, cols] * hist_s[k * n_seq:k * n_seq + tm, cols]
                acc = term if acc is None else acc + term
            tail_ref[:, cols] = hist_s[tm:tm + hist, cols]
            return acc

        def stage_scan(a, u, cols):
            staged[cols.start] = (a, u)

        def run_scan(cols):
            a, u = staged.pop(cols.start)
            h = h0_ref[:, cols]
            hs = []
            for t in range(n_t):
                rows = slice(t * n_seq, (t + 1) * n_seq)
                h = a[rows, :] * h + u[rows, :]
                hs.append(h)
            hs_ref[:, cols] = h
            return jnp.concatenate(hs, axis=0)

        _mixer_tile(
            xs_ref, ys_ref, *common,
            conv_lru=lambda cur, cols: causal_conv(cur, xl_s, cw_ref, LRU_CONV_W, lb_ref,
                                                   lcs_ref, cols),
            conv_sc=lambda cur, cols: causal_conv(cur, ch_s, sw_ref, SC_CONV_W, sb_ref,
                                                  scs_ref, cols),
            stage_scan=stage_scan, run_scan=run_scan)


def _output_kernel(x_ref, y_ref, p_ref, wout_ref, wp_ref, wgate_ref, bg_ref, lng_ref, lnb_ref,
                   o_ref, wout_buf, wp_buf, wgate_buf, *, alpha, n_w):
    step = pl.program_id(0)

    @pl.when(step < n_w)
    def _():
        _convert_weight_rows(wout_ref, wout_buf, step)
        _convert_weight_rows(wgate_ref, wgate_buf, step)

    @pl.when(step == 0)
    def _():
        _convert_weight_rows(wp_ref, wp_buf, 0)

    @pl.when(step >= n_w)
    def _():
        tb = x_ref.shape[0]
        everything = slice(None)
        halves = [slice(0, tb // 2), slice(tb // 2, tb)]
        ms = [jnp.dot(y_ref[h, :], _bf16_operand(wout_buf, everything),
                      preferred_element_type=F32) for h in halves]
        es = [jnp.dot(p_ref[h, :].astype(BF16), _bf16_operand(wp_buf, everything),
                      preferred_element_type=F32) for h in halves]
        rs = [alpha * x_ref[h, :] + m for h, m in zip(halves, ms)]
        gs = [jnp.dot(r.astype(BF16), _bf16_operand(wgate_buf, everything),
                      preferred_element_type=F32) for r in rs]
        for h, r, e, g in zip(halves, rs, es, gs):
            s = r + _sigmoid(g + bg_ref[...]) * e
            mu = jnp.mean(s, axis=-1, keepdims=True)
            d = s - mu
            var = jnp.mean(d * d, axis=-1, keepdims=True)
            o_ref[h, :] = d * lax.rsqrt(var + LN_EPS) * lng_ref[...] + lnb_ref[...]


def _resident(shape, layer):
    nd = len(shape)
    return pl.BlockSpec((None,) + tuple(shape), lambda *_: (layer,) + (0,) * nd,
                        pipeline_mode=pl.Buffered(1))


def _weight_rows(rows, n_cols, layer):
    last = D_MODEL // rows - 1
    return pl.BlockSpec((None, rows, n_cols), lambda s: (layer, jnp.minimum(s, last), 0))


def _mixer(xp, xs, h0, lb, sb, params, layer, batch, seq, n_seq, n_t):
    tm = MIX_TILE
    tiles_per_seq = seq // tm
    n_p = batch * tiles_per_seq
    n_s = (n_seq * n_t) // tm
    n_w = D_MODEL // MIX_W_ROWS
    seq_per_tile = tm // n_t
    assert seq % tm == 0 and (n_seq * n_t) % tm == 0 and tm % n_t == 0

    prompt_tile = lambda s: jnp.clip(s - n_w, 0, n_p - 1)
    sample_tile = lambda s: jnp.clip(s - n_w - n_p, 0, n_s - 1)
    p_tok = pl.BlockSpec((tm, D_MODEL), lambda s: (prompt_tile(s), 0))
    s_tok = pl.BlockSpec((tm, D_MODEL), lambda s: (sample_tile(s), 0))
    p_state = lambda k: pl.BlockSpec((None, k, D_LRU),
                                     lambda s: (prompt_tile(s) // tiles_per_seq, 0, 0))
    s_state_in = lambda k: pl.BlockSpec((None, k * seq_per_tile, D_LRU),
                                        lambda s: (layer, sample_tile(s), 0))
    s_state_out = lambda k: pl.BlockSpec((k * seq_per_tile, D_LRU), lambda s: (sample_tile(s), 0))

    (w_in, cw, cb, wg, ba, bx, lam, sw, gnl, gns) = params
    return pl.pallas_call(
        functools.partial(_mixer_kernel, n_w=n_w, n_p=n_p, tiles_per_seq=tiles_per_seq, n_t=n_t),
        grid=(n_w + n_p + n_s,),
        in_specs=[p_tok, s_tok, s_state_in(1), s_state_in(LRU_CONV_W - 1),
                  s_state_in(SC_CONV_W - 1),
                  _weight_rows(MIX_W_ROWS, N_PROJ * D_LRU, layer),
                  _resident((LRU_CONV_W, D_LRU), layer),
                  _resident((1, D_LRU), layer),
                  _resident((LRU_HEADS, HEAD_DIM, 2 * HEAD_DIM), layer),
                  _resident((1, D_LRU), layer),
                  _resident((1, D_LRU), layer),
                  _resident((1, D_LRU), layer),
                  _resident((SC_CONV_W, D_SC), layer),
                  _resident((1, D_LRU), layer),
                  _resident((1, D_SC), layer)],
        out_specs=[p_tok, s_tok, p_state(1), p_state(LRU_CONV_W - 1), p_state(SC_CONV_W - 1),
                   s_state_out(1), s_state_out(LRU_CONV_W - 1), s_state_out(SC_CONV_W - 1)],
        out_shape=[jax.ShapeDtypeStruct((batch * seq, D_MODEL), BF16),
                   jax.ShapeDtypeStruct((n_seq * n_t, D_MODEL), BF16),
                   jax.ShapeDtypeStruct((batch, 1, D_LRU), F32),
                   jax.ShapeDtypeStruct((batch, LRU_CONV_W - 1, D_LRU), F32),
                   jax.ShapeDtypeStruct((batch, SC_CONV_W - 1, D_SC), F32),
                   jax.ShapeDtypeStruct((n_seq, D_LRU), F32),
                   jax.ShapeDtypeStruct(((LRU_CONV_W - 1) * n_seq, D_LRU), F32),
                   jax.ShapeDtypeStruct(((SC_CONV_W - 1) * n_seq, D_SC), F32)],
        scratch_shapes=[pltpu.VMEM((D_MODEL // 2, N_PROJ * D_LRU), jnp.uint32),
                        pltpu.VMEM(((LRU_CONV_W - 1) * seq_per_tile + tm, D_LRU), F32),
                        pltpu.VMEM(((LRU_CONV_W - 1) * seq_per_tile + tm, D_SC), F32),
                        pltpu.VMEM((1, D_LRU), F32),
                        pltpu.VMEM((tm, D_MODEL), BF16)],
        compiler_params=pltpu.CompilerParams(
            dimension_semantics=("arbitrary",),
            vmem_limit_bytes=VMEM_LIMIT_BYTES),
        name=f"mixer_l{layer}",
    )(xp, xs, h0, lb, sb, w_in, cw, cb, wg, ba, bx, lam, sw, gnl, gns)


def _output(x, y, p, params, layer, alpha, tag):
    n = x.shape[0]
    tb = min(OUT_TILE, n)
    n_w = D_MODEL // OUT_W_ROWS
    assert n % tb == 0
    tile_of = lambda s: jnp.maximum(s - n_w, 0)
    tile = lambda w: pl.BlockSpec((tb, w), lambda s: (tile_of(s), 0))
    (w_out, wp, wgate, bg, lng, lnb) = params
    return pl.pallas_call(
        functools.partial(_output_kernel, alpha=alpha, n_w=n_w),
        grid=(n_w + n // tb,),
        in_specs=[tile(D_MODEL), tile(D_MODEL),
                  pl.BlockSpec((None, tb, D_PLE), lambda s: (layer, tile_of(s), 0)),
                  _weight_rows(OUT_W_ROWS, D_MODEL, layer),
                  _resident((D_PLE, D_MODEL), layer),
                  _weight_rows(OUT_W_ROWS, D_MODEL, layer),
                  _resident((1, D_MODEL), layer),
                  _resident((1, D_MODEL), layer),
                  _resident((1, D_MODEL), layer)],
        out_specs=tile(D_MODEL),
        out_shape=jax.ShapeDtypeStruct((n, D_MODEL), F32),
        scratch_shapes=[pltpu.VMEM((D_MODEL // 2, D_MODEL), jnp.uint32),
                        pltpu.VMEM((D_PLE // 2, D_MODEL), jnp.uint32),
                        pltpu.VMEM((D_MODEL // 2, D_MODEL), jnp.uint32)],
        compiler_params=pltpu.CompilerParams(
            dimension_semantics=("arbitrary",),
            vmem_limit_bytes=VMEM_LIMIT_BYTES),
        name=f"output_{tag}_l{layer}",
    )(x, y, p, w_out, wp, wgate, bg, lng, lnb)


def kernel(x_prompt, x_sample, state_lru_h, state_lru_conv, state_sc_conv, p_prompt, p_sample,
           w_in, lru_conv_w, lru_conv_b, lru_wa, lru_ba, lru_wx, lru_bx, lru_lambda,
           sc_conv_w, gn_lru, gn_sc, w_out, ple_wp, ple_wg, ple_bg, ln_g, ln_b):
    depth = w_in.shape[0]
    batch, seq, _ = x_prompt.shape
    n_seq, n_t, _ = x_sample.shape
    alpha = (2.0 * depth) ** 0.25
    seq_per_tile = MIX_TILE // n_t
    n_s = n_seq // seq_per_tile

    row = lambda v: v.reshape(depth, 1, -1)
    mixer_params = (
        w_in, lru_conv_w, row(lru_conv_b),
        jnp.concatenate([lru_wa, lru_wx], axis=-1).astype(BF16),
        row(lru_ba), row(lru_bx), row(lru_lambda), sc_conv_w, row(gn_lru), row(gn_sc))
    out_params = (w_out, ple_wp, ple_wg, row(ple_bg), row(ln_g), row(ln_b))

    def to_slabs(v):
        *lead, n, k, c = v.shape
        v = jnp.swapaxes(v.reshape(*lead, n_s, seq_per_tile, k, c), -3, -2)
        return v.reshape(*lead, n * k, c)

    def from_slabs(v, k):
        c = v.shape[-1]
        v = jnp.swapaxes(v.reshape(n_s, k, seq_per_tile, c), 1, 2)
        return v.reshape(n_seq, k, c)

    xs = to_slabs(x_sample)
    ps = to_slabs(p_sample)
    lb = to_slabs(state_lru_conv)
    sb = to_slabs(state_sc_conv)

    xp = x_prompt.reshape(batch * seq, D_MODEL)
    pp = p_prompt.reshape(depth, batch * seq, D_PLE)

    hp, lcp, scp, hsm, lcs, scs = [], [], [], [], [], []
    for l in range(depth):
        yp, ys, h_p, lc_p, sc_p, h_s, lc_s, sc_s = _mixer(
            xp, xs, state_lru_h, lb, sb, mixer_params, l, batch, seq, n_seq, n_t)
        xp = _output(xp, yp, pp, out_params, l, alpha, "prompt")
        xs = _output(xs, ys, ps, out_params, l, alpha, "sample")
        hp.append(h_p.reshape(batch, D_LRU)); lcp.append(lc_p); scp.append(sc_p)
        hsm.append(h_s)
        lcs.append(from_slabs(lc_s, LRU_CONV_W - 1))
        scs.append(from_slabs(sc_s, SC_CONV_W - 1))

    y_prompt = xp.reshape(batch, seq, D_MODEL)
    y_sample = from_slabs(xs, n_t)
    return (y_prompt, y_sample, jnp.stack(hp), jnp.stack(lcp), jnp.stack(scp),
            jnp.stack(hsm), jnp.stack(lcs), jnp.stack(scs))
```

```python
import functools

import jax
import jax.numpy as jnp
from jax import lax
from jax.experimental import pallas as pl
from jax.experimental.pallas import tpu as pltpu

D_MODEL = 2048
D_LRU = 1024
D_SC = 1024
LRU_HEADS = 8
HEAD_DIM = D_LRU // LRU_HEADS
SC_GROUPS = 8
LRU_CONV_W = 4
SC_CONV_W = 3
RG_LRU_C = 8.0
D_PLE = 256
N_PROJ = 6
LN_EPS = 1e-5
GN_EPS = 1e-6

SUBLANES = 8
MIX_TILE = 256
MIX_CHUNK = 256
OUT_TILE = 512
MIX_W_ROWS = 128
OUT_W_ROWS = 256
VMEM_LIMIT_BYTES = 60 * 1024 * 1024

F32 = jnp.float32
BF16 = jnp.bfloat16


def _sigmoid(v):
    return 1.0 / (1.0 + jnp.exp(-v))


def _silu(v):
    return v * _sigmoid(v)


def _group_rmsnorm(y, gain, n_groups):
    width = y.shape[1] // n_groups
    parts = []
    for g in range(n_groups):
        yg = y[:, g * width:(g + 1) * width]
        ms = jnp.mean(yg * yg, axis=-1, keepdims=True)
        parts.append(yg * lax.rsqrt(ms + GN_EPS))
    return jnp.concatenate(parts, axis=1) * gain


def _lru_gates(xc, wg_ref, heads, ba, bx):
    r_parts, i_parts = [], []
    for n, h in enumerate(heads):
        xh = xc[:, n * HEAD_DIM:(n + 1) * HEAD_DIM].astype(BF16)
        ri = jnp.dot(xh, wg_ref[h], preferred_element_type=F32)
        r_parts.append(ri[:, :HEAD_DIM])
        i_parts.append(ri[:, HEAD_DIM:])
    r = _sigmoid(jnp.concatenate(r_parts, axis=1) + ba)
    i = _sigmoid(jnp.concatenate(i_parts, axis=1) + bx)
    return r, i


def _head_rows(b_ref, heads):
    return jnp.concatenate([b_ref[h:h + 1, :] for h in heads], axis=1)


def _lru_coeffs(xc, r, i, lam):
    log_a = r * (-RG_LRU_C * jax.nn.softplus(-lam))
    a = jnp.exp(log_a)
    mult = jnp.sqrt(jnp.maximum(-jnp.tanh(log_a) * (a * a + 1.0), 0.0))
    return a, mult * (i * xc)


def _convert_weight_rows(w_ref, wbuf, step):
    half = w_ref.shape[0] // 2
    start = step * half if isinstance(step, int) else pl.multiple_of(step * half, half)
    wbuf[pl.ds(start, half), :] = pltpu.bitcast(w_ref[...].astype(BF16), jnp.uint32)


def _bf16_operand(wbuf, cols):
    return pltpu.bitcast(wbuf[:, cols], BF16)


def _mixer_tile(x_ref, y_ref, xb_s, wbuf, cb_ref, wg_ref, ba_ref, bx_ref, lam_ref, gnl_ref,
                gns_ref, conv_lru, conv_sc, stage_scan, run_scan):
    xb_s[...] = x_ref[...].astype(BF16)

    def proj(j, cols):
        lo = j * D_LRU + cols.start
        return jnp.dot(xb_s[...], _bf16_operand(wbuf, slice(lo, lo + MIX_CHUNK)),
                       preferred_element_type=F32)

    for c in range(D_LRU // MIX_CHUNK):
        cols = slice(c * MIX_CHUNK, (c + 1) * MIX_CHUNK)
        heads = range(cols.start // HEAD_DIM, cols.stop // HEAD_DIM)
        sc_cols = slice(D_LRU + cols.start, D_LRU + cols.stop)

        xl = proj(0, cols)
        c_pre, h_pre = proj(3, cols), proj(4, cols)
        xc = conv_lru(xl, cols) + cb_ref[:, cols]
        r, i = _lru_gates(xc, wg_ref, heads, _head_rows(ba_ref, heads), _head_rows(bx_ref, heads))
        b_pre, gs_pre = proj(2, cols), proj(5, cols)
        a, u = _lru_coeffs(xc, r, i, lam_ref[:, cols])
        stage_scan(a, u, cols)

        v = conv_sc(c_pre * h_pre, cols)
        y_s = b_pre * v * _silu(gs_pre)
        y_ref[:, sc_cols] = _group_rmsnorm(y_s, gns_ref[:, cols], len(heads)).astype(y_ref.dtype)

        gl_pre = proj(1, cols)
        y_l = run_scan(cols) * _silu(gl_pre)
        y_ref[:, cols] = _group_rmsnorm(y_l, gnl_ref[:, cols], len(heads)).astype(y_ref.dtype)


def _mixer_kernel(xp_ref, xs_ref, h0_ref, lb_ref, sb_ref, win_ref, cw_ref, cb_ref, wa_ref, wx_ref,
                  ba_ref, bx_ref, lam_ref, sw_ref, gnl_ref, gns_ref,
                  yp_ref, ys_ref, hp_ref, lcp_ref, scp_ref, hs_ref, lcs_ref, scs_ref,
                  wbuf, wg_s, xl_s, ch_s, h_s, xb_s, *, layer, n_w, n_p, tiles_per_seq, n_t):
    step = pl.program_id(0)
    tm = xp_ref.shape[0]
    this = slice(layer, layer + 1)
    common = (xb_s, wbuf, cb_ref.at[this], wg_s, ba_ref.at[layer], bx_ref.at[layer],
              lam_ref.at[this], gnl_ref.at[this], gns_ref.at[this])

    @pl.when(step < n_w)
    def _():
        _convert_weight_rows(win_ref, wbuf, step)

    @pl.when(step == 0)
    def _():
        for h in range(LRU_HEADS):
            wg_s[h, :, 0:HEAD_DIM] = wa_ref[h].astype(BF16)
            wg_s[h, :, HEAD_DIM:] = wx_ref[h].astype(BF16)

    @pl.when(jnp.logical_and(step >= n_w, step < n_w + n_p))
    def _():
        hdr = SUBLANES

        @pl.when(lax.rem(step - n_w, tiles_per_seq) == 0)
        def _():
            xl_s[0:hdr, :] = jnp.zeros((hdr, D_LRU), F32)
            ch_s[0:hdr, :] = jnp.zeros((hdr, D_SC), F32)
            h_s[...] = jnp.zeros_like(h_s)

        def causal_conv(cur, hist_s, w_ref, width, tail_ref, cols):
            n_blk = tm // SUBLANES
            row = lax.broadcasted_iota(jnp.int32, (SUBLANES, cur.shape[1]), 0)
            blocks = [hist_s[0:hdr, cols]] + [
                cur[i * SUBLANES:(i + 1) * SUBLANES, :] for i in range(n_blk)]
            acc = w_ref[width - 1:width, cols] * cur
            for d in range(1, width):
                rolled = [pltpu.roll(b, d, 0) for b in blocks]
                shifted = jnp.concatenate(
                    [jnp.where(row < d, rolled[i], rolled[i + 1]) for i in range(n_blk)], axis=0)
                acc = acc + w_ref[width - 1 - d:width - d, cols] * shifted
            hist_s[0:hdr, cols] = cur[tm - hdr:tm, :]
            tail_ref[:, cols] = cur[tm - (width - 1):tm, :]
            return acc

        staged = {}

        def stage_scan(a, u, cols):
            staged[cols.start] = (a, u)

        def run_scan(cols):
            a, u = staged.pop(cols.start)
            row = lax.broadcasted_iota(jnp.int32, (SUBLANES, a.shape[1]), 0)
            last = slice(SUBLANES - 1, SUBLANES)
            cums = []
            for blk in range(tm // SUBLANES):
                rows = slice(blk * SUBLANES, (blk + 1) * SUBLANES)
                a_cum, u_cum = a[rows, :], u[rows, :]
                d = 1
                while d < SUBLANES:
                    has_prev = row >= d
                    a_prev = jnp.where(has_prev, pltpu.roll(a_cum, d, 0), 1.0)
                    u_prev = jnp.where(has_prev, pltpu.roll(u_cum, d, 0), 0.0)
                    u_cum = a_cum * u_prev + u_cum
                    a_cum = a_cum * a_prev
                    d *= 2
                cums.append((a_cum, u_cum))
            h = h_s[:, cols]
            h_in = []
            for a_cum, u_cum in cums:
                h_in.append(h)
                h = a_cum[last, :] * h + u_cum[last, :]
            h_s[:, cols] = h
            hp_ref[:, cols] = h
            return jnp.concatenate(
                [a_cum * h0 + u_cum for (a_cum, u_cum), h0 in zip(cums, h_in)], axis=0)

        _mixer_tile(
            xp_ref, yp_ref, *common,
            conv_lru=lambda cur, cols: causal_conv(cur, xl_s, cw_ref, LRU_CONV_W, lcp_ref, cols),
            conv_sc=lambda cur, cols: causal_conv(cur, ch_s, sw_ref, SC_CONV_W, scp_ref, cols),
            stage_scan=stage_scan, run_scan=run_scan)

    @pl.when(step >= n_w + n_p)
    def _():
        n_seq = tm // n_t
        staged = {}

        def causal_conv(cur, hist_s, w_ref, width, buf_ref, tail_ref, cols):
            hist = (width - 1) * n_seq
            hist_s[0:hist, cols] = buf_ref[:, cols]
            hist_s[hist:hist + tm, cols] = cur
            acc = None
            for k in range(width):
                term = w_ref[k:k + 1, cols] * hist_s[k * n_seq:k * n_seq + tm, cols]
                acc = term if acc is None else acc + term
            tail_ref[:, cols] = hist_s[tm:tm + hist, cols]
            return acc

        def stage_scan(a, u, cols):
            staged[cols.start] = (a, u)

        def run_scan(cols):
            a, u = staged.pop(cols.start)
            h = h0_ref[:, cols]
            hs = []
            for t in range(n_t):
                rows = slice(t * n_seq, (t + 1) * n_seq)
                h = a[rows, :] * h + u[rows, :]
                hs.append(h)
            hs_ref[:, cols] = h
            return jnp.concatenate(hs, axis=0)

        _mixer_tile(
            xs_ref, ys_ref, *common,
            conv_lru=lambda cur, cols: causal_conv(cur, xl_s, cw_ref, LRU_CONV_W, lb_ref,
                                                   lcs_ref, cols),
            conv_sc=lambda cur, cols: causal_conv(cur, ch_s, sw_ref, SC_CONV_W, sb_ref,
                                                  scs_ref, cols),
            stage_scan=stage_scan, run_scan=run_scan)


def _output_kernel(x_ref, y_ref, p_ref, wout_ref, wp_ref, wgate_ref, bg_ref, lng_ref, lnb_ref,
                   o_ref, wout_buf, wp_buf, wgate_buf, *, alpha, n_w, layer):
    step = pl.program_id(0)
    this = slice(layer, layer + 1)

    @pl.when(step < n_w)
    def _():
        _convert_weight_rows(wout_ref, wout_buf, step)
        _convert_weight_rows(wgate_ref, wgate_buf, step)

    @pl.when(step == 0)
    def _():
        _convert_weight_rows(wp_ref, wp_buf, 0)

    @pl.when(step >= n_w)
    def _():
        tb = x_ref.shape[0]
        everything = slice(None)
        halves = [slice(0, tb // 2), slice(tb // 2, tb)]
        ms = [jnp.dot(y_ref[h, :], _bf16_operand(wout_buf, everything),
                      preferred_element_type=F32) for h in halves]
        es = [jnp.dot(p_ref[h, :].astype(BF16), _bf16_operand(wp_buf, everything),
                      preferred_element_type=F32) for h in halves]
        rs = [alpha * x_ref[h, :] + m for h, m in zip(halves, ms)]
        gs = [jnp.dot(r.astype(BF16), _bf16_operand(wgate_buf, everything),
                      preferred_element_type=F32) for r in rs]
        for h, r, e, g in zip(halves, rs, es, gs):
            s = r + _sigmoid(g + bg_ref[this, :]) * e
            mu = jnp.mean(s, axis=-1, keepdims=True)
            d = s - mu
            var = jnp.mean(d * d, axis=-1, keepdims=True)
            o_ref[h, :] = d * lax.rsqrt(var + LN_EPS) * lng_ref[this, :] + lnb_ref[this, :]


def _resident(shape, layer):
    nd = len(shape)
    return pl.BlockSpec((None,) + tuple(shape), lambda *_: (layer,) + (0,) * nd,
                        pipeline_mode=pl.Buffered(1))


def _stacked(arr):
    return pl.BlockSpec(arr.shape, lambda *_: (0,) * arr.ndim, pipeline_mode=pl.Buffered(1))


def _weight_rows(rows, n_cols, layer):
    last = D_MODEL // rows - 1
    return pl.BlockSpec((None, rows, n_cols), lambda s: (layer, jnp.minimum(s, last), 0))


def _mixer(xp, xs, h0, lb, sb, params, layer, batch, seq, n_seq, n_t):
    tm = MIX_TILE
    tiles_per_seq = seq // tm
    n_p = batch * tiles_per_seq
    n_s = (n_seq * n_t) // tm
    n_w = D_MODEL // MIX_W_ROWS
    seq_per_tile = tm // n_t
    assert seq % tm == 0 and (n_seq * n_t) % tm == 0 and tm % n_t == 0

    prompt_tile = lambda s: jnp.clip(s - n_w, 0, n_p - 1)
    sample_tile = lambda s: jnp.clip(s - n_w - n_p, 0, n_s - 1)
    p_tok = pl.BlockSpec((tm, D_MODEL), lambda s: (prompt_tile(s), 0))
    s_tok = pl.BlockSpec((tm, D_MODEL), lambda s: (sample_tile(s), 0))
    p_state = lambda k: pl.BlockSpec((None, k, D_LRU),
                                     lambda s: (prompt_tile(s) // tiles_per_seq, 0, 0))
    s_state_in = lambda k: pl.BlockSpec((None, k * seq_per_tile, D_LRU),
                                        lambda s: (layer, sample_tile(s), 0))
    s_state_out = lambda k: pl.BlockSpec((k * seq_per_tile, D_LRU), lambda s: (sample_tile(s), 0))

    (w_in, cw, cb, wa, wx, ba, bx, lam, sw, gnl, gns) = params
    return pl.pallas_call(
        functools.partial(_mixer_kernel, layer=layer, n_w=n_w, n_p=n_p,
                          tiles_per_seq=tiles_per_seq, n_t=n_t),
        grid=(n_w + n_p + n_s,),
        in_specs=[p_tok, s_tok, s_state_in(1), s_state_in(LRU_CONV_W - 1),
                  s_state_in(SC_CONV_W - 1),
                  _weight_rows(MIX_W_ROWS, N_PROJ * D_LRU, layer),
                  _resident((LRU_CONV_W, D_LRU), layer),
                  _stacked(cb),
                  _resident((LRU_HEADS, HEAD_DIM, HEAD_DIM), layer),
                  _resident((LRU_HEADS, HEAD_DIM, HEAD_DIM), layer),
                  _stacked(ba), _stacked(bx), _stacked(lam),
                  _resident((SC_CONV_W, D_SC), layer),
                  _stacked(gnl), _stacked(gns)],
        out_specs=[p_tok, s_tok, p_state(1), p_state(LRU_CONV_W - 1), p_state(SC_CONV_W - 1),
                   s_state_out(1), s_state_out(LRU_CONV_W - 1), s_state_out(SC_CONV_W - 1)],
        out_shape=[jax.ShapeDtypeStruct((batch * seq, D_MODEL), BF16),
                   jax.ShapeDtypeStruct((n_seq * n_t, D_MODEL), BF16),
                   jax.ShapeDtypeStruct((batch, 1, D_LRU), F32),
                   jax.ShapeDtypeStruct((batch, LRU_CONV_W - 1, D_LRU), F32),
                   jax.ShapeDtypeStruct((batch, SC_CONV_W - 1, D_SC), F32),
                   jax.ShapeDtypeStruct((n_seq, D_LRU), F32),
                   jax.ShapeDtypeStruct(((LRU_CONV_W - 1) * n_seq, D_LRU), F32),
                   jax.ShapeDtypeStruct(((SC_CONV_W - 1) * n_seq, D_SC), F32)],
        scratch_shapes=[pltpu.VMEM((D_MODEL // 2, N_PROJ * D_LRU), jnp.uint32),
                        pltpu.VMEM((LRU_HEADS, HEAD_DIM, 2 * HEAD_DIM), BF16),
                        pltpu.VMEM(((LRU_CONV_W - 1) * seq_per_tile + tm, D_LRU), F32),
                        pltpu.VMEM(((LRU_CONV_W - 1) * seq_per_tile + tm, D_SC), F32),
                        pltpu.VMEM((1, D_LRU), F32),
                        pltpu.VMEM((tm, D_MODEL), BF16)],
        compiler_params=pltpu.CompilerParams(
            dimension_semantics=("arbitrary",),
            vmem_limit_bytes=VMEM_LIMIT_BYTES),
        name=f"mixer_l{layer}",
    )(xp, xs, h0, lb, sb, w_in, cw, cb, wa, wx, ba, bx, lam, sw, gnl, gns)


def _output(x, y, p, params, layer, alpha, tag):
    n = x.shape[0]
    tb = min(OUT_TILE, n)
    n_w = D_MODEL // OUT_W_ROWS
    assert n % tb == 0
    tile_of = lambda s: jnp.maximum(s - n_w, 0)
    tile = lambda w: pl.BlockSpec((tb, w), lambda s: (tile_of(s), 0))
    (w_out, wp, wgate, bg, lng, lnb) = params
    return pl.pallas_call(
        functools.partial(_output_kernel, alpha=alpha, n_w=n_w, layer=layer),
        grid=(n_w + n // tb,),
        in_specs=[tile(D_MODEL), tile(D_MODEL),
                  pl.BlockSpec((None, tb, D_PLE), lambda s: (layer, tile_of(s), 0)),
                  _weight_rows(OUT_W_ROWS, D_MODEL, layer),
                  _resident((D_PLE, D_MODEL), layer),
                  _weight_rows(OUT_W_ROWS, D_MODEL, layer),
                  _stacked(bg), _stacked(lng), _stacked(lnb)],
        out_specs=tile(D_MODEL),
        out_shape=jax.ShapeDtypeStruct((n, D_MODEL), F32),
        scratch_shapes=[pltpu.VMEM((D_MODEL // 2, D_MODEL), jnp.uint32),
                        pltpu.VMEM((D_PLE // 2, D_MODEL), jnp.uint32),
                        pltpu.VMEM((D_MODEL // 2, D_MODEL), jnp.uint32)],
        compiler_params=pltpu.CompilerParams(
            dimension_semantics=("arbitrary",),
            vmem_limit_bytes=VMEM_LIMIT_BYTES),
        name=f"output_{tag}_l{layer}",
    )(x, y, p, w_out, wp, wgate, bg, lng, lnb)


def kernel(x_prompt, x_sample, state_lru_h, state_lru_conv, state_sc_conv, p_prompt, p_sample,
           w_in, lru_conv_w, lru_conv_b, lru_wa, lru_ba, lru_wx, lru_bx, lru_lambda,
           sc_conv_w, gn_lru, gn_sc, w_out, ple_wp, ple_wg, ple_bg, ln_g, ln_b):
    depth = w_in.shape[0]
    batch, seq, _ = x_prompt.shape
    n_seq, n_t, _ = x_sample.shape
    alpha = (2.0 * depth) ** 0.25
    seq_per_tile = MIX_TILE // n_t
    n_s = n_seq // seq_per_tile

    mixer_params = (w_in, lru_conv_w, lru_conv_b, lru_wa, lru_wx, lru_ba, lru_bx, lru_lambda,
                    sc_conv_w, gn_lru, gn_sc)
    out_params = (w_out, ple_wp, ple_wg, ple_bg, ln_g, ln_b)

    def to_slabs(v):
        *lead, n, k, c = v.shape
        v = jnp.swapaxes(v.reshape(*lead, n_s, seq_per_tile, k, c), -3, -2)
        return v.reshape(*lead, n * k, c)

    def from_slabs(v, k):
        c = v.shape[-1]
        v = jnp.swapaxes(v.reshape(n_s, k, seq_per_tile, c), 1, 2)
        return v.reshape(n_seq, k, c)

    xs = to_slabs(x_sample)
    ps = to_slabs(p_sample)
    lb = to_slabs(state_lru_conv)
    sb = to_slabs(state_sc_conv)

    xp = x_prompt.reshape(batch * seq, D_MODEL)
    pp = p_prompt.reshape(depth, batch * seq, D_PLE)

    hp, lcp, scp, hsm, lcs, scs = [], [], [], [], [], []
    for l in range(depth):
        yp, ys, h_p, lc_p, sc_p, h_s, lc_s, sc_s = _mixer(
            xp, xs, state_lru_h, lb, sb, mixer_params, l, batch, seq, n_seq, n_t)
        xp = _output(xp, yp, pp, out_params, l, alpha, "prompt")
        xs = _output(xs, ys, ps, out_params, l, alpha, "sample")
        hp.append(h_p.reshape(batch, D_LRU)); lcp.append(lc_p); scp.append(sc_p)
        hsm.append(h_s)
        lcs.append(from_slabs(lc_s, LRU_CONV_W - 1))
        scs.append(from_slabs(sc_s, SC_CONV_W - 1))

    y_prompt = xp.reshape(batch, seq, D_MODEL)
    y_sample = from_slabs(xs, n_t)
    return (y_prompt, y_sample, jnp.stack(hp), jnp.stack(lcp), jnp.stack(scp),
            jnp.stack(hsm), jnp.stack(lcs), jnp.stack(scs))
```

```python
import functools

import jax
import jax.numpy as jnp
from jax import lax
from jax.experimental import pallas as pl
from jax.experimental.pallas import tpu as pltpu

D_MODEL = 2048
D_LRU = 1024
D_SC = 1024
LRU_HEADS = 8
HEAD_DIM = D_LRU // LRU_HEADS
SC_GROUPS = 8
LRU_CONV_W = 4
SC_CONV_W = 3
RG_LRU_C = 8.0
D_PLE = 256
N_PROJ = 6
LN_EPS = 1e-5
GN_EPS = 1e-6

SUBLANES = 8
MIX_TILE = 256
PROMPT_TILES_PER_STEP = 2
MIX_CHUNK = 256
OUT_TILE = 512
MIX_W_ROWS = 128
OUT_W_ROWS = 256
VMEM_LIMIT_BYTES = 60 * 1024 * 1024

F32 = jnp.float32
BF16 = jnp.bfloat16


def _sigmoid(v):
    return 1.0 / (1.0 + jnp.exp(-v))


def _silu(v):
    return v * _sigmoid(v)


def _group_rmsnorm(y, gain, n_groups):
    width = y.shape[1] // n_groups
    parts = []
    for g in range(n_groups):
        yg = y[:, g * width:(g + 1) * width]
        ms = jnp.mean(yg * yg, axis=-1, keepdims=True)
        parts.append(yg * lax.rsqrt(ms + GN_EPS))
    return jnp.concatenate(parts, axis=1) * gain


def _lru_gates(xc, wg_ref, heads, ba, bx):
    r_parts, i_parts = [], []
    for n, h in enumerate(heads):
        xh = xc[:, n * HEAD_DIM:(n + 1) * HEAD_DIM].astype(BF16)
        ri = jnp.dot(xh, wg_ref[h], preferred_element_type=F32)
        r_parts.append(ri[:, :HEAD_DIM])
        i_parts.append(ri[:, HEAD_DIM:])
    r = _sigmoid(jnp.concatenate(r_parts, axis=1) + ba)
    i = _sigmoid(jnp.concatenate(i_parts, axis=1) + bx)
    return r, i


def _head_rows(b_ref, heads):
    return jnp.concatenate([b_ref[h:h + 1, :] for h in heads], axis=1)


def _lru_coeffs(xc, r, i, lam):
    log_a = r * (-RG_LRU_C * jax.nn.softplus(-lam))
    a = jnp.exp(log_a)
    mult = jnp.sqrt(jnp.maximum(-jnp.tanh(log_a) * (a * a + 1.0), 0.0))
    return a, mult * (i * xc)


def _convert_weight_rows(w_ref, wbuf, step):
    half = w_ref.shape[0] // 2
    start = step * half if isinstance(step, int) else pl.multiple_of(step * half, half)
    wbuf[pl.ds(start, half), :] = pltpu.bitcast(w_ref[...].astype(BF16), jnp.uint32)


def _bf16_operand(wbuf, cols):
    return pltpu.bitcast(wbuf[:, cols], BF16)


def _mixer_tile(x_ref, y_ref, xb_s, wbuf, cb_ref, wg_ref, ba_ref, bx_ref, lam_ref, gnl_ref,
                gns_ref, conv_lru, conv_sc, stage_scan, run_scan):
    xb_s[...] = x_ref[...].astype(BF16)

    def proj(j, cols):
        lo = j * D_LRU + cols.start
        return jnp.dot(xb_s[...], _bf16_operand(wbuf, slice(lo, lo + MIX_CHUNK)),
                       preferred_element_type=F32)

    for c in range(D_LRU // MIX_CHUNK):
        cols = slice(c * MIX_CHUNK, (c + 1) * MIX_CHUNK)
        heads = range(cols.start // HEAD_DIM, cols.stop // HEAD_DIM)
        sc_cols = slice(D_LRU + cols.start, D_LRU + cols.stop)

        xl = proj(0, cols)
        c_pre, h_pre = proj(3, cols), proj(4, cols)
        xc = conv_lru(xl, cols) + cb_ref[:, cols]
        r, i = _lru_gates(xc, wg_ref, heads, _head_rows(ba_ref, heads), _head_rows(bx_ref, heads))
        b_pre, gs_pre = proj(2, cols), proj(5, cols)
        a, u = _lru_coeffs(xc, r, i, lam_ref[:, cols])
        stage_scan(a, u, cols)

        v = conv_sc(c_pre * h_pre, cols)
        y_s = b_pre * v * _silu(gs_pre)
        y_ref[:, sc_cols] = _group_rmsnorm(y_s, gns_ref[:, cols], len(heads)).astype(y_ref.dtype)

        gl_pre = proj(1, cols)
        y_l = run_scan(cols) * _silu(gl_pre)
        y_ref[:, cols] = _group_rmsnorm(y_l, gnl_ref[:, cols], len(heads)).astype(y_ref.dtype)


def _mixer_kernel(xp_ref, xs_ref, h0_ref, lb_ref, sb_ref, win_ref, cw_ref, cb_ref, wa_ref, wx_ref,
                  ba_ref, bx_ref, lam_ref, sw_ref, gnl_ref, gns_ref,
                  yp_ref, ys_ref, hp_ref, lcp_ref, scp_ref, hs_ref, lcs_ref, scs_ref,
                  wbuf, wg_s, xl_s, ch_s, h_s, xb_s, *, layer, n_w, n_p, steps_per_seq, n_t):
    step = pl.program_id(0)
    tm = xs_ref.shape[0]
    this = slice(layer, layer + 1)
    common = (wbuf, cb_ref.at[this], wg_s, ba_ref.at[layer], bx_ref.at[layer],
              lam_ref.at[this], gnl_ref.at[this], gns_ref.at[this])

    @pl.when(step < n_w)
    def _():
        _convert_weight_rows(win_ref, wbuf, step)

    @pl.when(step == 0)
    def _():
        for h in range(LRU_HEADS):
            wg_s[h, :, 0:HEAD_DIM] = wa_ref[h].astype(BF16)
            wg_s[h, :, HEAD_DIM:] = wx_ref[h].astype(BF16)

    @pl.when(jnp.logical_and(step >= n_w, step < n_w + n_p))
    def _():
        hdr = SUBLANES

        @pl.when(lax.rem(step - n_w, steps_per_seq) == 0)
        def _():
            xl_s[0:hdr, :] = jnp.zeros((hdr, D_LRU), F32)
            ch_s[0:hdr, :] = jnp.zeros((hdr, D_SC), F32)
            h_s[...] = jnp.zeros_like(h_s)

        def causal_conv(cur, hist_s, w_ref, width, tail_ref, cols):
            n_blk = tm // SUBLANES
            row = lax.broadcasted_iota(jnp.int32, (SUBLANES, cur.shape[1]), 0)
            blocks = [hist_s[0:hdr, cols]] + [
                cur[i * SUBLANES:(i + 1) * SUBLANES, :] for i in range(n_blk)]
            acc = w_ref[width - 1:width, cols] * cur
            for d in range(1, width):
                rolled = [pltpu.roll(b, d, 0) for b in blocks]
                shifted = jnp.concatenate(
                    [jnp.where(row < d, rolled[i], rolled[i + 1]) for i in range(n_blk)], axis=0)
                acc = acc + w_ref[width - 1 - d:width - d, cols] * shifted
            hist_s[0:hdr, cols] = cur[tm - hdr:tm, :]
            tail_ref[:, cols] = cur[tm - (width - 1):tm, :]
            return acc

        staged = {}

        def stage_scan(a, u, cols):
            staged[cols.start] = (a, u)

        def run_scan(cols):
            a, u = staged.pop(cols.start)
            row = lax.broadcasted_iota(jnp.int32, (SUBLANES, a.shape[1]), 0)
            last = slice(SUBLANES - 1, SUBLANES)
            cums = []
            for blk in range(tm // SUBLANES):
                rows = slice(blk * SUBLANES, (blk + 1) * SUBLANES)
                a_cum, u_cum = a[rows, :], u[rows, :]
                d = 1
                while d < SUBLANES:
                    has_prev = row >= d
                    a_prev = jnp.where(has_prev, pltpu.roll(a_cum, d, 0), 1.0)
                    u_prev = jnp.where(has_prev, pltpu.roll(u_cum, d, 0), 0.0)
                    u_cum = a_cum * u_prev + u_cum
                    a_cum = a_cum * a_prev
                    d *= 2
                cums.append((a_cum, u_cum))
            h = h_s[:, cols]
            h_in = []
            for a_cum, u_cum in cums:
                h_in.append(h)
                h = a_cum[last, :] * h + u_cum[last, :]
            h_s[:, cols] = h
            hp_ref[:, cols] = h
            return jnp.concatenate(
                [a_cum * h0 + u_cum for (a_cum, u_cum), h0 in zip(cums, h_in)], axis=0)

        for sub in range(xp_ref.shape[0] // tm):
            rows = pl.ds(sub * tm, tm)
            _mixer_tile(
                xp_ref.at[rows], yp_ref.at[rows], xb_s.at[sub], *common,
                conv_lru=lambda cur, cols: causal_conv(cur, xl_s, cw_ref, LRU_CONV_W, lcp_ref,
                                                       cols),
                conv_sc=lambda cur, cols: causal_conv(cur, ch_s, sw_ref, SC_CONV_W, scp_ref,
                                                      cols),
                stage_scan=stage_scan, run_scan=run_scan)

    @pl.when(step >= n_w + n_p)
    def _():
        n_seq = tm // n_t
        staged = {}

        def causal_conv(cur, w_ref, width, buf_ref, tail_ref, cols):
            hist = jnp.concatenate([buf_ref[:, cols], cur], axis=0)
            acc = None
            for k in range(width):
                term = w_ref[k:k + 1, cols] * hist[k * n_seq:k * n_seq + tm, :]
                acc = term if acc is None else acc + term
            tail_ref[:, cols] = hist[tm:, :]
            return acc

        def stage_scan(a, u, cols):
            staged[cols.start] = (a, u)

        def run_scan(cols):
            a, u = staged.pop(cols.start)
            h = h0_ref[:, cols]
            hs = []
            for t in range(n_t):
                rows = slice(t * n_seq, (t + 1) * n_seq)
                h = a[rows, :] * h + u[rows, :]
                hs.append(h)
            hs_ref[:, cols] = h
            return jnp.concatenate(hs, axis=0)

        _mixer_tile(
            xs_ref, ys_ref, xb_s.at[0], *common,
            conv_lru=lambda cur, cols: causal_conv(cur, cw_ref, LRU_CONV_W, lb_ref, lcs_ref, cols),
            conv_sc=lambda cur, cols: causal_conv(cur, sw_ref, SC_CONV_W, sb_ref, scs_ref, cols),
            stage_scan=stage_scan, run_scan=run_scan)


def _output_kernel(x_ref, y_ref, p_ref, wout_ref, wp_ref, wgate_ref, bg_ref, lng_ref, lnb_ref,
                   o_ref, wout_buf, wp_buf, wgate_buf, *, alpha, n_w, layer):
    step = pl.program_id(0)
    this = slice(layer, layer + 1)

    @pl.when(step < n_w)
    def _():
        _convert_weight_rows(wout_ref, wout_buf, step)
        _convert_weight_rows(wgate_ref, wgate_buf, step)

    @pl.when(step == 0)
    def _():
        _convert_weight_rows(wp_ref, wp_buf, 0)

    @pl.when(step >= n_w)
    def _():
        tb = x_ref.shape[0]
        everything = slice(None)
        halves = [slice(0, tb // 2), slice(tb // 2, tb)]
        ms = [jnp.dot(y_ref[h, :], _bf16_operand(wout_buf, everything),
                      preferred_element_type=F32) for h in halves]
        es = [jnp.dot(p_ref[h, :].astype(BF16), _bf16_operand(wp_buf, everything),
                      preferred_element_type=F32) for h in halves]
        rs = [alpha * x_ref[h, :] + m for h, m in zip(halves, ms)]
        gs = [jnp.dot(r.astype(BF16), _bf16_operand(wgate_buf, everything),
                      preferred_element_type=F32) for r in rs]
        for h, r, e, g in zip(halves, rs, es, gs):
            s = r + _sigmoid(g + bg_ref[this, :]) * e
            mu = jnp.mean(s, axis=-1, keepdims=True)
            d = s - mu
            var = jnp.mean(d * d, axis=-1, keepdims=True)
            o_ref[h, :] = d * lax.rsqrt(var + LN_EPS) * lng_ref[this, :] + lnb_ref[this, :]


def _resident(shape, layer):
    nd = len(shape)
    return pl.BlockSpec((None,) + tuple(shape), lambda *_: (layer,) + (0,) * nd,
                        pipeline_mode=pl.Buffered(1))


def _stacked(arr):
    return pl.BlockSpec(arr.shape, lambda *_: (0,) * arr.ndim, pipeline_mode=pl.Buffered(1))


def _weight_rows(rows, n_cols, layer):
    last = D_MODEL // rows - 1
    return pl.BlockSpec((None, rows, n_cols), lambda s: (layer, jnp.minimum(s, last), 0))


def _mixer(xp, xs, h0, lb, sb, params, layer, batch, seq, n_seq, n_t):
    tm = MIX_TILE
    p_rows = tm * PROMPT_TILES_PER_STEP
    steps_per_seq = seq // p_rows
    n_p = batch * steps_per_seq
    n_s = (n_seq * n_t) // tm
    n_w = D_MODEL // MIX_W_ROWS
    seq_per_tile = tm // n_t
    assert seq % p_rows == 0 and (n_seq * n_t) % tm == 0 and tm % n_t == 0

    prompt_step = lambda s: jnp.clip(s - n_w, 0, n_p - 1)
    sample_tile = lambda s: jnp.clip(s - n_w - n_p, 0, n_s - 1)
    once = pl.Buffered(1)
    p_tok = pl.BlockSpec((p_rows, D_MODEL), lambda s: (prompt_step(s), 0))
    s_tok = lambda **kw: pl.BlockSpec((tm, D_MODEL), lambda s: (sample_tile(s), 0), **kw)
    p_state = lambda k: pl.BlockSpec((None, k, D_LRU),
                                     lambda s: (prompt_step(s) // steps_per_seq, 0, 0))
    s_state_in = lambda k: pl.BlockSpec((None, k * seq_per_tile, D_LRU),
                                        lambda s: (layer, sample_tile(s), 0), pipeline_mode=once)
    s_state_out = lambda k: pl.BlockSpec((k * seq_per_tile, D_LRU), lambda s: (sample_tile(s), 0))

    (w_in, cw, cb, wa, wx, ba, bx, lam, sw, gnl, gns) = params
    return pl.pallas_call(
        functools.partial(_mixer_kernel, layer=layer, n_w=n_w, n_p=n_p,
                          steps_per_seq=steps_per_seq, n_t=n_t),
        grid=(n_w + n_p + n_s,),
        in_specs=[p_tok, s_tok(pipeline_mode=once), s_state_in(1), s_state_in(LRU_CONV_W - 1),
                  s_state_in(SC_CONV_W - 1),
                  _weight_rows(MIX_W_ROWS, N_PROJ * D_LRU, layer),
                  _resident((LRU_CONV_W, D_LRU), layer),
                  _stacked(cb),
                  _resident((LRU_HEADS, HEAD_DIM, HEAD_DIM), layer),
                  _resident((LRU_HEADS, HEAD_DIM, HEAD_DIM), layer),
                  _stacked(ba), _stacked(bx), _stacked(lam),
                  _resident((SC_CONV_W, D_SC), layer),
                  _stacked(gnl), _stacked(gns)],
        out_specs=[p_tok, s_tok(), p_state(1), p_state(LRU_CONV_W - 1), p_state(SC_CONV_W - 1),
                   s_state_out(1), s_state_out(LRU_CONV_W - 1), s_state_out(SC_CONV_W - 1)],
        out_shape=[jax.ShapeDtypeStruct((batch * seq, D_MODEL), BF16),
                   jax.ShapeDtypeStruct((n_seq * n_t, D_MODEL), BF16),
                   jax.ShapeDtypeStruct((batch, 1, D_LRU), F32),
                   jax.ShapeDtypeStruct((batch, LRU_CONV_W - 1, D_LRU), F32),
                   jax.ShapeDtypeStruct((batch, SC_CONV_W - 1, D_SC), F32),
                   jax.ShapeDtypeStruct((n_seq, D_LRU), F32),
                   jax.ShapeDtypeStruct(((LRU_CONV_W - 1) * n_seq, D_LRU), F32),
                   jax.ShapeDtypeStruct(((SC_CONV_W - 1) * n_seq, D_SC), F32)],
        scratch_shapes=[pltpu.VMEM((D_MODEL // 2, N_PROJ * D_LRU), jnp.uint32),
                        pltpu.VMEM((LRU_HEADS, HEAD_DIM, 2 * HEAD_DIM), BF16),
                        pltpu.VMEM((SUBLANES, D_LRU), F32),
                        pltpu.VMEM((SUBLANES, D_SC), F32),
                        pltpu.VMEM((1, D_LRU), F32),
                        pltpu.VMEM((PROMPT_TILES_PER_STEP, tm, D_MODEL), BF16)],
        compiler_params=pltpu.CompilerParams(
            dimension_semantics=("arbitrary",),
            vmem_limit_bytes=VMEM_LIMIT_BYTES),
        name=f"mixer_l{layer}",
    )(xp, xs, h0, lb, sb, w_in, cw, cb, wa, wx, ba, bx, lam, sw, gnl, gns)


def _output(x, y, p, params, layer, alpha, tag):
    n = x.shape[0]
    tb = min(OUT_TILE, n)
    n_w = D_MODEL // OUT_W_ROWS
    assert n % tb == 0
    tile_of = lambda s: jnp.maximum(s - n_w, 0)
    tile = lambda w: pl.BlockSpec((tb, w), lambda s: (tile_of(s), 0))
    (w_out, wp, wgate, bg, lng, lnb) = params
    return pl.pallas_call(
        functools.partial(_output_kernel, alpha=alpha, n_w=n_w, layer=layer),
        grid=(n_w + n // tb,),
        in_specs=[tile(D_MODEL), tile(D_MODEL),
                  pl.BlockSpec((None, tb, D_PLE), lambda s: (layer, tile_of(s), 0)),
                  _weight_rows(OUT_W_ROWS, D_MODEL, layer),
                  _resident((D_PLE, D_MODEL), layer),
                  _weight_rows(OUT_W_ROWS, D_MODEL, layer),
                  _stacked(bg), _stacked(lng), _stacked(lnb)],
        out_specs=tile(D_MODEL),
        out_shape=jax.ShapeDtypeStruct((n, D_MODEL), F32),
        scratch_shapes=[pltpu.VMEM((D_MODEL // 2, D_MODEL), jnp.uint32),
                        pltpu.VMEM((D_PLE // 2, D_MODEL), jnp.uint32),
                        pltpu.VMEM((D_MODEL // 2, D_MODEL), jnp.uint32)],
        compiler_params=pltpu.CompilerParams(
            dimension_semantics=("arbitrary",),
            vmem_limit_bytes=VMEM_LIMIT_BYTES),
        name=f"output_{tag}_l{layer}",
    )(x, y, p, w_out, wp, wgate, bg, lng, lnb)


def kernel(x_prompt, x_sample, state_lru_h, state_lru_conv, state_sc_conv, p_prompt, p_sample,
           w_in, lru_conv_w, lru_conv_b, lru_wa, lru_ba, lru_wx, lru_bx, lru_lambda,
           sc_conv_w, gn_lru, gn_sc, w_out, ple_wp, ple_wg, ple_bg, ln_g, ln_b):
    depth = w_in.shape[0]
    batch, seq, _ = x_prompt.shape
    n_seq, n_t, _ = x_sample.shape
    alpha = (2.0 * depth) ** 0.25
    seq_per_tile = MIX_TILE // n_t
    n_s = n_seq // seq_per_tile

    mixer_params = (w_in, lru_conv_w, lru_conv_b, lru_wa, lru_wx, lru_ba, lru_bx, lru_lambda,
                    sc_conv_w, gn_lru, gn_sc)
    out_params = (w_out, ple_wp, ple_wg, ple_bg, ln_g, ln_b)

    def to_slabs(v):
        *lead, n, k, c = v.shape
        v = jnp.swapaxes(v.reshape(*lead, n_s, seq_per_tile, k, c), -3, -2)
        return v.reshape(*lead, n * k, c)

    def from_slabs(v, k):
        c = v.shape[-1]
        v = jnp.swapaxes(v.reshape(n_s, k, seq_per_tile, c), 1, 2)
        return v.reshape(n_seq, k, c)

    xs = to_slabs(x_sample)
    ps = to_slabs(p_sample)
    lb = to_slabs(state_lru_conv)
    sb = to_slabs(state_sc_conv)

    xp = x_prompt.reshape(batch * seq, D_MODEL)
    pp = p_prompt.reshape(depth, batch * seq, D_PLE)

    hp, lcp, scp, hsm, lcs, scs = [], [], [], [], [], []
    for l in range(depth):
        yp, ys, h_p, lc_p, sc_p, h_s, lc_s, sc_s = _mixer(
            xp, xs, state_lru_h, lb, sb, mixer_params, l, batch, seq, n_seq, n_t)
        xp = _output(xp, yp, pp, out_params, l, alpha, "prompt")
        xs = _output(xs, ys, ps, out_params, l, alpha, "sample")
        hp.append(h_p.reshape(batch, D_LRU)); lcp.append(lc_p); scp.append(sc_p)
        hsm.append(h_s)
        lcs.append(from_slabs(lc_s, LRU_CONV_W - 1))
        scs.append(from_slabs(sc_s, SC_CONV_W - 1))

    y_prompt = xp.reshape(batch, seq, D_MODEL)
    y_sample = from_slabs(xs, n_t)
    return (y_prompt, y_sample, jnp.stack(hp), jnp.stack(lcp), jnp.stack(scp),
            jnp.stack(hsm), jnp.stack(lcs), jnp.stack(scs))
```

```python
import functools

import jax
import jax.numpy as jnp
from jax import lax
from jax.experimental import pallas as pl
from jax.experimental.pallas import tpu as pltpu

D_MODEL = 2048
D_LRU = 1024
D_SC = 1024
LRU_HEADS = 8
HEAD_DIM = D_LRU // LRU_HEADS
SC_GROUPS = 8
LRU_CONV_W = 4
SC_CONV_W = 3
RG_LRU_C = 8.0
D_PLE = 256
N_PROJ = 6
LN_EPS = 1e-5
GN_EPS = 1e-6

SUBLANES = 8
MIX_TILE = 256
PROMPT_TILES_PER_STEP = 1
MIX_CHUNK = 256
OUT_TILE = 512
MIX_W_ROWS = 256
OUT_W_ROWS = 256
VMEM_LIMIT_BYTES = 60 * 1024 * 1024

F32 = jnp.float32
BF16 = jnp.bfloat16


def _sigmoid(v):
    return 1.0 / (1.0 + jnp.exp(-v))


def _silu(v):
    return v * _sigmoid(v)


def _group_rmsnorm(y, gain, n_groups):
    width = y.shape[1] // n_groups
    parts = []
    for g in range(n_groups):
        yg = y[:, g * width:(g + 1) * width]
        ms = jnp.mean(yg * yg, axis=-1, keepdims=True)
        parts.append(yg * lax.rsqrt(ms + GN_EPS))
    return jnp.concatenate(parts, axis=1) * gain


def _lru_gates(xc, wg_ref, heads, ba, bx):
    r_parts, i_parts = [], []
    for n, h in enumerate(heads):
        xh = xc[:, n * HEAD_DIM:(n + 1) * HEAD_DIM].astype(BF16)
        ri = jnp.dot(xh, wg_ref[h], preferred_element_type=F32)
        r_parts.append(ri[:, :HEAD_DIM])
        i_parts.append(ri[:, HEAD_DIM:])
    r = _sigmoid(jnp.concatenate(r_parts, axis=1) + ba)
    i = _sigmoid(jnp.concatenate(i_parts, axis=1) + bx)
    return r, i


def _head_rows(b_ref, heads):
    return jnp.concatenate([b_ref[h:h + 1, :] for h in heads], axis=1)


def _lru_coeffs(xc, r, i, lam):
    log_a = r * (-RG_LRU_C * jax.nn.softplus(-lam))
    a = jnp.exp(log_a)
    mult = jnp.sqrt(jnp.maximum(-jnp.tanh(log_a) * (a * a + 1.0), 0.0))
    return a, mult * (i * xc)


def _convert_weight_rows(w_ref, wbuf, step):
    half = w_ref.shape[0] // 2
    start = step * half if isinstance(step, int) else pl.multiple_of(step * half, half)
    wbuf[pl.ds(start, half), :] = pltpu.bitcast(w_ref[...].astype(BF16), jnp.uint32)


def _bf16_operand(wbuf, cols):
    return pltpu.bitcast(wbuf[:, cols], BF16)


def _mixer_tile(x_ref, y_ref, xb_s, wbuf, cb_ref, wg_ref, ba_ref, bx_ref, lam_ref, gnl_ref,
                gns_ref, conv_lru, conv_sc, stage_scan, run_scan):
    xb_s[...] = x_ref[...].astype(BF16)

    def proj(j, cols):
        lo = j * D_LRU + cols.start
        return jnp.dot(xb_s[...], _bf16_operand(wbuf, slice(lo, lo + MIX_CHUNK)),
                       preferred_element_type=F32)

    for c in range(D_LRU // MIX_CHUNK):
        cols = slice(c * MIX_CHUNK, (c + 1) * MIX_CHUNK)
        heads = range(cols.start // HEAD_DIM, cols.stop // HEAD_DIM)
        sc_cols = slice(D_LRU + cols.start, D_LRU + cols.stop)

        xl = proj(0, cols)
        c_pre, h_pre = proj(3, cols), proj(4, cols)
        xc = conv_lru(xl, cols) + cb_ref[:, cols]
        r, i = _lru_gates(xc, wg_ref, heads, _head_rows(ba_ref, heads), _head_rows(bx_ref, heads))
        b_pre, gs_pre = proj(2, cols), proj(5, cols)
        a, u = _lru_coeffs(xc, r, i, lam_ref[:, cols])
        stage_scan(a, u, cols)

        v = conv_sc(c_pre * h_pre, cols)
        y_s = b_pre * v * _silu(gs_pre)
        y_ref[:, sc_cols] = _group_rmsnorm(y_s, gns_ref[:, cols], len(heads)).astype(y_ref.dtype)

        gl_pre = proj(1, cols)
        y_l = run_scan(cols) * _silu(gl_pre)
        y_ref[:, cols] = _group_rmsnorm(y_l, gnl_ref[:, cols], len(heads)).astype(y_ref.dtype)


def _mixer_kernel(xp_ref, xs_ref, h0_ref, lb_ref, sb_ref, win_ref, cw_ref, cb_ref, wa_ref, wx_ref,
                  ba_ref, bx_ref, lam_ref, sw_ref, gnl_ref, gns_ref,
                  yp_ref, ys_ref, hp_ref, lcp_ref, scp_ref, hs_ref, lcs_ref, scs_ref,
                  wbuf, wg_s, xl_s, ch_s, h_s, xb_s, *, layer, n_w, n_p, steps_per_seq, n_t):
    step = pl.program_id(0)
    tm = xs_ref.shape[0]
    this = slice(layer, layer + 1)
    common = (wbuf, cb_ref.at[this], wg_s, ba_ref.at[layer], bx_ref.at[layer],
              lam_ref.at[this], gnl_ref.at[this], gns_ref.at[this])

    @pl.when(step < n_w)
    def _():
        _convert_weight_rows(win_ref, wbuf, step)

    @pl.when(step == 0)
    def _():
        for h in range(LRU_HEADS):
            wg_s[h, :, 0:HEAD_DIM] = wa_ref[h].astype(BF16)
            wg_s[h, :, HEAD_DIM:] = wx_ref[h].astype(BF16)

    @pl.when(jnp.logical_and(step >= n_w, step < n_w + n_p))
    def _():
        hdr = SUBLANES

        @pl.when(lax.rem(step - n_w, steps_per_seq) == 0)
        def _():
            xl_s[0:hdr, :] = jnp.zeros((hdr, D_LRU), F32)
            ch_s[0:hdr, :] = jnp.zeros((hdr, D_SC), F32)
            h_s[...] = jnp.zeros_like(h_s)

        def causal_conv(cur, hist_s, w_ref, width, tail_ref, cols):
            n_blk = tm // SUBLANES
            row = lax.broadcasted_iota(jnp.int32, (SUBLANES, cur.shape[1]), 0)
            blocks = [hist_s[0:hdr, cols]] + [
                cur[i * SUBLANES:(i + 1) * SUBLANES, :] for i in range(n_blk)]
            acc = w_ref[width - 1:width, cols] * cur
            for d in range(1, width):
                rolled = [pltpu.roll(b, d, 0) for b in blocks]
                shifted = jnp.concatenate(
                    [jnp.where(row < d, rolled[i], rolled[i + 1]) for i in range(n_blk)], axis=0)
                acc = acc + w_ref[width - 1 - d:width - d, cols] * shifted
            hist_s[0:hdr, cols] = cur[tm - hdr:tm, :]
            tail_ref[:, cols] = cur[tm - (width - 1):tm, :]
            return acc

        staged = {}

        def stage_scan(a, u, cols):
            staged[cols.start] = (a, u)

        def run_scan(cols):
            a, u = staged.pop(cols.start)
            row = lax.broadcasted_iota(jnp.int32, (SUBLANES, a.shape[1]), 0)
            last = slice(SUBLANES - 1, SUBLANES)
            cums = []
            for blk in range(tm // SUBLANES):
                rows = slice(blk * SUBLANES, (blk + 1) * SUBLANES)
                a_cum, u_cum = a[rows, :], u[rows, :]
                d = 1
                while d < SUBLANES:
                    has_prev = row >= d
                    a_prev = jnp.where(has_prev, pltpu.roll(a_cum, d, 0), 1.0)
                    u_prev = jnp.where(has_prev, pltpu.roll(u_cum, d, 0), 0.0)
                    u_cum = a_cum * u_prev + u_cum
                    a_cum = a_cum * a_prev
                    d *= 2
                cums.append((a_cum, u_cum))
            h = h_s[:, cols]
            h_in = []
            for a_cum, u_cum in cums:
                h_in.append(h)
                h = a_cum[last, :] * h + u_cum[last, :]
            h_s[:, cols] = h
            hp_ref[:, cols] = h
            return jnp.concatenate(
                [a_cum * h0 + u_cum for (a_cum, u_cum), h0 in zip(cums, h_in)], axis=0)

        for sub in range(xp_ref.shape[0] // tm):
            rows = pl.ds(sub * tm, tm)
            _mixer_tile(
                xp_ref.at[rows], yp_ref.at[rows], xb_s.at[sub], *common,
                conv_lru=lambda cur, cols: causal_conv(cur, xl_s, cw_ref, LRU_CONV_W, lcp_ref,
                                                       cols),
                conv_sc=lambda cur, cols: causal_conv(cur, ch_s, sw_ref, SC_CONV_W, scp_ref,
                                                      cols),
                stage_scan=stage_scan, run_scan=run_scan)

    @pl.when(step >= n_w + n_p)
    def _():
        n_seq = tm // n_t
        staged = {}

        def causal_conv(cur, w_ref, width, buf_ref, tail_ref, cols):
            hist = jnp.concatenate([buf_ref[:, cols], cur], axis=0)
            acc = None
            for k in range(width):
                term = w_ref[k:k + 1, cols] * hist[k * n_seq:k * n_seq + tm, :]
                acc = term if acc is None else acc + term
            tail_ref[:, cols] = hist[tm:, :]
            return acc

        def stage_scan(a, u, cols):
            staged[cols.start] = (a, u)

        def run_scan(cols):
            a, u = staged.pop(cols.start)
            h = h0_ref[:, cols]
            hs = []
            for t in range(n_t):
                rows = slice(t * n_seq, (t + 1) * n_seq)
                h = a[rows, :] * h + u[rows, :]
                hs.append(h)
            hs_ref[:, cols] = h
            return jnp.concatenate(hs, axis=0)

        _mixer_tile(
            xs_ref, ys_ref, xb_s.at[0], *common,
            conv_lru=lambda cur, cols: causal_conv(cur, cw_ref, LRU_CONV_W, lb_ref, lcs_ref, cols),
            conv_sc=lambda cur, cols: causal_conv(cur, sw_ref, SC_CONV_W, sb_ref, scs_ref, cols),
            stage_scan=stage_scan, run_scan=run_scan)


def _output_kernel(x_ref, y_ref, p_ref, wout_ref, wp_ref, wgate_ref, bg_ref, lng_ref, lnb_ref,
                   o_ref, wout_buf, wp_buf, wgate_buf, *, alpha, n_w, layer):
    step = pl.program_id(0)
    this = slice(layer, layer + 1)

    @pl.when(step < n_w)
    def _():
        _convert_weight_rows(wout_ref, wout_buf, step)
        _convert_weight_rows(wgate_ref, wgate_buf, step)

    @pl.when(step == 0)
    def _():
        _convert_weight_rows(wp_ref, wp_buf, 0)

    @pl.when(step >= n_w)
    def _():
        tb = x_ref.shape[0]
        everything = slice(None)
        halves = [slice(0, 320), slice(320, tb)]
        ms = [jnp.dot(y_ref[h, :], _bf16_operand(wout_buf, everything),
                      preferred_element_type=F32) for h in halves]
        es = [jnp.dot(p_ref[h, :].astype(BF16), _bf16_operand(wp_buf, everything),
                      preferred_element_type=F32) for h in halves]
        rs = [alpha * x_ref[h, :] + m for h, m in zip(halves, ms)]
        gs = [jnp.dot(r.astype(BF16), _bf16_operand(wgate_buf, everything),
                      preferred_element_type=F32) for r in rs]
        for h, r, e, g in zip(halves, rs, es, gs):
            s = r + _sigmoid(g + bg_ref[this, :]) * e
            mu = jnp.mean(s, axis=-1, keepdims=True)
            d = s - mu
            var = jnp.mean(d * d, axis=-1, keepdims=True)
            o_ref[h, :] = d * lax.rsqrt(var + LN_EPS) * lng_ref[this, :] + lnb_ref[this, :]


def _resident(shape, layer):
    nd = len(shape)
    return pl.BlockSpec((None,) + tuple(shape), lambda *_: (layer,) + (0,) * nd,
                        pipeline_mode=pl.Buffered(1))


def _stacked(arr):
    return pl.BlockSpec(arr.shape, lambda *_: (0,) * arr.ndim, pipeline_mode=pl.Buffered(1))


def _weight_rows(rows, n_cols, layer):
    last = D_MODEL // rows - 1
    return pl.BlockSpec((None, rows, n_cols), lambda s: (layer, jnp.minimum(s, last), 0))


def _mixer(xp, xs, h0, lb, sb, params, layer, batch, seq, n_seq, n_t):
    tm = MIX_TILE
    p_rows = tm * PROMPT_TILES_PER_STEP
    steps_per_seq = seq // p_rows
    n_p = batch * steps_per_seq
    n_s = (n_seq * n_t) // tm
    n_w = D_MODEL // MIX_W_ROWS
    seq_per_tile = tm // n_t
    assert seq % p_rows == 0 and (n_seq * n_t) % tm == 0 and tm % n_t == 0

    prompt_step = lambda s: jnp.clip(s - n_w, 0, n_p - 1)
    sample_tile = lambda s: jnp.clip(s - n_w - n_p, 0, n_s - 1)
    once = pl.Buffered(1)
    p_tok = pl.BlockSpec((p_rows, D_MODEL), lambda s: (prompt_step(s), 0))
    s_tok = lambda **kw: pl.BlockSpec((tm, D_MODEL), lambda s: (sample_tile(s), 0), **kw)
    p_state = lambda k: pl.BlockSpec((None, k, D_LRU),
                                     lambda s: (prompt_step(s) // steps_per_seq, 0, 0))
    s_state_in = lambda k: pl.BlockSpec((None, k * seq_per_tile, D_LRU),
                                        lambda s: (layer, sample_tile(s), 0), pipeline_mode=once)
    s_state_out = lambda k: pl.BlockSpec((k * seq_per_tile, D_LRU), lambda s: (sample_tile(s), 0))

    (w_in, cw, cb, wa, wx, ba, bx, lam, sw, gnl, gns) = params
    return pl.pallas_call(
        functools.partial(_mixer_kernel, layer=layer, n_w=n_w, n_p=n_p,
                          steps_per_seq=steps_per_seq, n_t=n_t),
        grid=(n_w + n_p + n_s,),
        in_specs=[p_tok, s_tok(pipeline_mode=once), s_state_in(1), s_state_in(LRU_CONV_W - 1),
                  s_state_in(SC_CONV_W - 1),
                  _weight_rows(MIX_W_ROWS, N_PROJ * D_LRU, layer),
                  _resident((LRU_CONV_W, D_LRU), layer),
                  _stacked(cb),
                  _resident((LRU_HEADS, HEAD_DIM, HEAD_DIM), layer),
                  _resident((LRU_HEADS, HEAD_DIM, HEAD_DIM), layer),
                  _stacked(ba), _stacked(bx), _stacked(lam),
                  _resident((SC_CONV_W, D_SC), layer),
                  _stacked(gnl), _stacked(gns)],
        out_specs=[p_tok, s_tok(), p_state(1), p_state(LRU_CONV_W - 1), p_state(SC_CONV_W - 1),
                   s_state_out(1), s_state_out(LRU_CONV_W - 1), s_state_out(SC_CONV_W - 1)],
        out_shape=[jax.ShapeDtypeStruct((batch * seq, D_MODEL), BF16),
                   jax.ShapeDtypeStruct((n_seq * n_t, D_MODEL), BF16),
                   jax.ShapeDtypeStruct((batch, 1, D_LRU), F32),
                   jax.ShapeDtypeStruct((batch, LRU_CONV_W - 1, D_LRU), F32),
                   jax.ShapeDtypeStruct((batch, SC_CONV_W - 1, D_SC), F32),
                   jax.ShapeDtypeStruct((n_seq, D_LRU), F32),
                   jax.ShapeDtypeStruct(((LRU_CONV_W - 1) * n_seq, D_LRU), F32),
                   jax.ShapeDtypeStruct(((SC_CONV_W - 1) * n_seq, D_SC), F32)],
        scratch_shapes=[pltpu.VMEM((D_MODEL // 2, N_PROJ * D_LRU), jnp.uint32),
                        pltpu.VMEM((LRU_HEADS, HEAD_DIM, 2 * HEAD_DIM), BF16),
                        pltpu.VMEM((SUBLANES, D_LRU), F32),
                        pltpu.VMEM((SUBLANES, D_SC), F32),
                        pltpu.VMEM((1, D_LRU), F32),
                        pltpu.VMEM((PROMPT_TILES_PER_STEP, tm, D_MODEL), BF16)],
        compiler_params=pltpu.CompilerParams(
            dimension_semantics=("arbitrary",),
            vmem_limit_bytes=VMEM_LIMIT_BYTES),
        name=f"mixer_l{layer}",
    )(xp, xs, h0, lb, sb, w_in, cw, cb, wa, wx, ba, bx, lam, sw, gnl, gns)


def _output(x, y, p, params, layer, alpha, tag):
    n = x.shape[0]
    tb = min(OUT_TILE, n)
    n_w = D_MODEL // OUT_W_ROWS
    assert n % tb == 0
    tile_of = lambda s: jnp.maximum(s - n_w, 0)
    tile = lambda w: pl.BlockSpec((tb, w), lambda s: (tile_of(s), 0))
    (w_out, wp, wgate, bg, lng, lnb) = params
    return pl.pallas_call(
        functools.partial(_output_kernel, alpha=alpha, n_w=n_w, layer=layer),
        grid=(n_w + n // tb,),
        in_specs=[tile(D_MODEL), tile(D_MODEL),
                  pl.BlockSpec((None, tb, D_PLE), lambda s: (layer, tile_of(s), 0)),
                  _weight_rows(OUT_W_ROWS, D_MODEL, layer),
                  _resident((D_PLE, D_MODEL), layer),
                  _weight_rows(OUT_W_ROWS, D_MODEL, layer),
                  _stacked(bg), _stacked(lng), _stacked(lnb)],
        out_specs=tile(D_MODEL),
        out_shape=jax.ShapeDtypeStruct((n, D_MODEL), F32),
        scratch_shapes=[pltpu.VMEM((D_MODEL // 2, D_MODEL), jnp.uint32),
                        pltpu.VMEM((D_PLE // 2, D_MODEL), jnp.uint32),
                        pltpu.VMEM((D_MODEL // 2, D_MODEL), jnp.uint32)],
        compiler_params=pltpu.CompilerParams(
            dimension_semantics=("arbitrary",),
            vmem_limit_bytes=VMEM_LIMIT_BYTES),
        name=f"output_{tag}_l{layer}",
    )(x, y, p, w_out, wp, wgate, bg, lng, lnb)


def kernel(x_prompt, x_sample, state_lru_h, state_lru_conv, state_sc_conv, p_prompt, p_sample,
           w_in, lru_conv_w, lru_conv_b, lru_wa, lru_ba, lru_wx, lru_bx, lru_lambda,
           sc_conv_w, gn_lru, gn_sc, w_out, ple_wp, ple_wg, ple_bg, ln_g, ln_b):
    depth = w_in.shape[0]
    batch, seq, _ = x_prompt.shape
    n_seq, n_t, _ = x_sample.shape
    alpha = (2.0 * depth) ** 0.25
    seq_per_tile = MIX_TILE // n_t
    n_s = n_seq // seq_per_tile

    mixer_params = (w_in, lru_conv_w, lru_conv_b, lru_wa, lru_wx, lru_ba, lru_bx, lru_lambda,
                    sc_conv_w, gn_lru, gn_sc)
    out_params = (w_out, ple_wp, ple_wg, ple_bg, ln_g, ln_b)

    def to_slabs(v):
        *lead, n, k, c = v.shape
        v = jnp.swapaxes(v.reshape(*lead, n_s, seq_per_tile, k, c), -3, -2)
        return v.reshape(*lead, n * k, c)

    def from_slabs(v, k):
        c = v.shape[-1]
        v = jnp.swapaxes(v.reshape(n_s, k, seq_per_tile, c), 1, 2)
        return v.reshape(n_seq, k, c)

    xs = to_slabs(x_sample)
    ps = to_slabs(p_sample)
    lb = to_slabs(state_lru_conv)
    sb = to_slabs(state_sc_conv)

    xp = x_prompt.reshape(batch * seq, D_MODEL)
    pp = p_prompt.reshape(depth, batch * seq, D_PLE)

    hp, lcp, scp, hsm, lcs, scs = [], [], [], [], [], []
    for l in range(depth):
        yp, ys, h_p, lc_p, sc_p, h_s, lc_s, sc_s = _mixer(
            xp, xs, state_lru_h, lb, sb, mixer_params, l, batch, seq, n_seq, n_t)
        xp = _output(xp, yp, pp, out_params, l, alpha, "prompt")
        xs = _output(xs, ys, ps, out_params, l, alpha, "sample")
        hp.append(h_p.reshape(batch, D_LRU)); lcp.append(lc_p); scp.append(sc_p)
        hsm.append(h_s)
        lcs.append(from_slabs(lc_s, LRU_CONV_W - 1))
        scs.append(from_slabs(sc_s, SC_CONV_W - 1))

    y_prompt = xp.reshape(batch, seq, D_MODEL)
    y_sample = from_slabs(xs, n_t)
    return (y_prompt, y_sample, jnp.stack(hp), jnp.stack(lcp), jnp.stack(scp),
            jnp.stack(hsm), jnp.stack(lcs), jnp.stack(scs))
```

```python
import functools

import jax
import jax.numpy as jnp
from jax import lax
from jax.experimental import pallas as pl
from jax.experimental.pallas import tpu as pltpu

D_MODEL = 2048
D_LRU = 1024
D_SC = 1024
LRU_HEADS = 8
HEAD_DIM = D_LRU // LRU_HEADS
SC_GROUPS = 8
LRU_CONV_W = 4
SC_CONV_W = 3
RG_LRU_C = 8.0
D_PLE = 256
N_PROJ = 6
LN_EPS = 1e-5
GN_EPS = 1e-6
NEG_LOG2_E = -1.4426950408889634

SUBLANES = 8
MIX_TILE = 256
PROMPT_TILES_PER_STEP = 1
MIX_CHUNK = 256
OUT_TILE = 512
OUT_FIRST_GROUP_EIGHTHS = 5
MIX_W_ROWS = 256
OUT_W_ROWS = 256
VMEM_LIMIT_BYTES = 60 * 1024 * 1024

F32 = jnp.float32
BF16 = jnp.bfloat16


def _sigmoid(v):
    return 1.0 / (1.0 + jnp.exp2(v * NEG_LOG2_E))


def _silu(v):
    return v * _sigmoid(v)


def _group_rmsnorm(y, gain, n_groups):
    width = y.shape[1] // n_groups
    parts = []
    for g in range(n_groups):
        yg = y[:, g * width:(g + 1) * width]
        ms = jnp.mean(yg * yg, axis=-1, keepdims=True)
        parts.append(yg * lax.rsqrt(ms + GN_EPS))
    return jnp.concatenate(parts, axis=1) * gain


def _lru_gates(xc, wg_ref, heads, ba, bx):
    r_parts, i_parts = [], []
    for n, h in enumerate(heads):
        xh = xc[:, n * HEAD_DIM:(n + 1) * HEAD_DIM].astype(BF16)
        ri = jnp.dot(xh, wg_ref[h], preferred_element_type=F32)
        r_parts.append(ri[:, :HEAD_DIM])
        i_parts.append(ri[:, HEAD_DIM:])
    r = _sigmoid(jnp.concatenate(r_parts, axis=1) + ba)
    i = _sigmoid(jnp.concatenate(i_parts, axis=1) + bx)
    return r, i


def _head_rows(b_ref, heads):
    return jnp.concatenate([b_ref[h:h + 1, :] for h in heads], axis=1)


def _lru_coeffs(xc, r, i, lam):
    log_a = r * (-RG_LRU_C * jax.nn.softplus(-lam))
    a = jnp.exp(log_a)
    mult = jnp.sqrt(jnp.maximum(-jnp.tanh(log_a) * (a * a + 1.0), 0.0))
    return a, mult * (i * xc)


def _convert_weight_rows(w_ref, wbuf, step):
    half = w_ref.shape[0] // 2
    start = step * half if isinstance(step, int) else pl.multiple_of(step * half, half)
    wbuf[pl.ds(start, half), :] = pltpu.bitcast(w_ref[...].astype(BF16), jnp.uint32)


def _bf16_operand(wbuf, cols):
    return pltpu.bitcast(wbuf[:, cols], BF16)


def _mixer_tile(x_ref, y_ref, xb_s, wbuf, cb_ref, wg_ref, ba_ref, bx_ref, lam_ref, gnl_ref,
                gns_ref, conv_lru, conv_sc, stage_scan, run_scan):
    xb_s[...] = x_ref[...].astype(BF16)

    def proj(j, cols):
        lo = j * D_LRU + cols.start
        return jnp.dot(xb_s[...], _bf16_operand(wbuf, slice(lo, lo + MIX_CHUNK)),
                       preferred_element_type=F32)

    for c in range(D_LRU // MIX_CHUNK):
        cols = slice(c * MIX_CHUNK, (c + 1) * MIX_CHUNK)
        heads = range(cols.start // HEAD_DIM, cols.stop // HEAD_DIM)
        sc_cols = slice(D_LRU + cols.start, D_LRU + cols.stop)

        xl = proj(0, cols)
        c_pre, h_pre = proj(3, cols), proj(4, cols)
        xc = conv_lru(xl, cols) + cb_ref[:, cols]
        r, i = _lru_gates(xc, wg_ref, heads, _head_rows(ba_ref, heads), _head_rows(bx_ref, heads))
        b_pre, gs_pre = proj(2, cols), proj(5, cols)
        a, u = _lru_coeffs(xc, r, i, lam_ref[:, cols])
        stage_scan(a, u, cols)

        v = conv_sc(c_pre * h_pre, cols)
        y_s = b_pre * v * _silu(gs_pre)
        y_ref[:, sc_cols] = _group_rmsnorm(y_s, gns_ref[:, cols], len(heads)).astype(y_ref.dtype)

        gl_pre = proj(1, cols)
        y_l = run_scan(cols) * _silu(gl_pre)
        y_ref[:, cols] = _group_rmsnorm(y_l, gnl_ref[:, cols], len(heads)).astype(y_ref.dtype)


def _mixer_kernel(xp_ref, xs_ref, h0_ref, lb_ref, sb_ref, win_ref, cw_ref, cb_ref, wa_ref, wx_ref,
                  ba_ref, bx_ref, lam_ref, sw_ref, gnl_ref, gns_ref,
                  yp_ref, ys_ref, hp_ref, lcp_ref, scp_ref, hs_ref, lcs_ref, scs_ref,
                  wbuf, wg_s, xl_s, ch_s, h_s, xb_s, *, layer, n_w, n_p, steps_per_seq, n_t):
    step = pl.program_id(0)
    tm = xs_ref.shape[0]
    this = slice(layer, layer + 1)
    common = (wbuf, cb_ref.at[this], wg_s, ba_ref.at[layer], bx_ref.at[layer],
              lam_ref.at[this], gnl_ref.at[this], gns_ref.at[this])

    @pl.when(step < n_w)
    def _():
        _convert_weight_rows(win_ref, wbuf, step)

    @pl.when(step == 0)
    def _():
        for h in range(LRU_HEADS):
            wg_s[h, :, 0:HEAD_DIM] = wa_ref[h].astype(BF16)
            wg_s[h, :, HEAD_DIM:] = wx_ref[h].astype(BF16)

    @pl.when(jnp.logical_and(step >= n_w, step < n_w + n_p))
    def _():
        hdr = SUBLANES

        @pl.when(lax.rem(step - n_w, steps_per_seq) == 0)
        def _():
            xl_s[0:hdr, :] = jnp.zeros((hdr, D_LRU), F32)
            ch_s[0:hdr, :] = jnp.zeros((hdr, D_SC), F32)
            h_s[...] = jnp.zeros_like(h_s)

        def causal_conv(cur, hist_s, w_ref, width, tail_ref, cols):
            n_blk = tm // SUBLANES
            row = lax.broadcasted_iota(jnp.int32, (SUBLANES, cur.shape[1]), 0)
            blocks = [hist_s[0:hdr, cols]] + [
                cur[i * SUBLANES:(i + 1) * SUBLANES, :] for i in range(n_blk)]
            acc = w_ref[width - 1:width, cols] * cur
            for d in range(1, width):
                rolled = [pltpu.roll(b, d, 0) for b in blocks]
                shifted = jnp.concatenate(
                    [jnp.where(row < d, rolled[i], rolled[i + 1]) for i in range(n_blk)], axis=0)
                acc = acc + w_ref[width - 1 - d:width - d, cols] * shifted
            hist_s[0:hdr, cols] = cur[tm - hdr:tm, :]
            tail_ref[:, cols] = cur[tm - (width - 1):tm, :]
            return acc

        staged = {}

        def stage_scan(a, u, cols):
            staged[cols.start] = (a, u)

        def run_scan(cols):
            a, u = staged.pop(cols.start)
            row = lax.broadcasted_iota(jnp.int32, (SUBLANES, a.shape[1]), 0)
            last = slice(SUBLANES - 1, SUBLANES)
            cums = []
            for blk in range(tm // SUBLANES):
                rows = slice(blk * SUBLANES, (blk + 1) * SUBLANES)
                a_cum, u_cum = a[rows, :], u[rows, :]
                d = 1
                while d < SUBLANES:
                    has_prev = row >= d
                    a_prev = jnp.where(has_prev, pltpu.roll(a_cum, d, 0), 1.0)
                    u_prev = jnp.where(has_prev, pltpu.roll(u_cum, d, 0), 0.0)
                    u_cum = a_cum * u_prev + u_cum
                    a_cum = a_cum * a_prev
                    d *= 2
                cums.append((a_cum, u_cum))
            h = h_s[:, cols]
            h_in = []
            for a_cum, u_cum in cums:
                h_in.append(h)
                h = a_cum[last, :] * h + u_cum[last, :]
            h_s[:, cols] = h
            hp_ref[:, cols] = h
            return jnp.concatenate(
                [a_cum * h0 + u_cum for (a_cum, u_cum), h0 in zip(cums, h_in)], axis=0)

        for sub in range(xp_ref.shape[0] // tm):
            rows = pl.ds(sub * tm, tm)
            _mixer_tile(
                xp_ref.at[rows], yp_ref.at[rows], xb_s.at[sub], *common,
                conv_lru=lambda cur, cols: causal_conv(cur, xl_s, cw_ref, LRU_CONV_W, lcp_ref,
                                                       cols),
                conv_sc=lambda cur, cols: causal_conv(cur, ch_s, sw_ref, SC_CONV_W, scp_ref,
                                                      cols),
                stage_scan=stage_scan, run_scan=run_scan)

    @pl.when(step >= n_w + n_p)
    def _():
        n_seq = tm // n_t
        staged = {}

        def causal_conv(cur, w_ref, width, buf_ref, tail_ref, cols):
            hist = jnp.concatenate([buf_ref[:, cols], cur], axis=0)
            acc = None
            for k in range(width):
                term = w_ref[k:k + 1, cols] * hist[k * n_seq:k * n_seq + tm, :]
                acc = term if acc is None else acc + term
            tail_ref[:, cols] = hist[tm:, :]
            return acc

        def stage_scan(a, u, cols):
            staged[cols.start] = (a, u)

        def run_scan(cols):
            a, u = staged.pop(cols.start)
            h = h0_ref[:, cols]
            hs = []
            for t in range(n_t):
                rows = slice(t * n_seq, (t + 1) * n_seq)
                h = a[rows, :] * h + u[rows, :]
                hs.append(h)
            hs_ref[:, cols] = h
            return jnp.concatenate(hs, axis=0)

        _mixer_tile(
            xs_ref, ys_ref, xb_s.at[0], *common,
            conv_lru=lambda cur, cols: causal_conv(cur, cw_ref, LRU_CONV_W, lb_ref, lcs_ref, cols),
            conv_sc=lambda cur, cols: causal_conv(cur, sw_ref, SC_CONV_W, sb_ref, scs_ref, cols),
            stage_scan=stage_scan, run_scan=run_scan)


def _output_kernel(x_ref, y_ref, p_ref, wout_ref, wp_ref, wgate_ref, bg_ref, lng_ref, lnb_ref,
                   o_ref, wout_buf, wp_buf, wgate_buf, *, alpha, n_w, layer):
    step = pl.program_id(0)
    this = slice(layer, layer + 1)

    @pl.when(step < n_w)
    def _():
        _convert_weight_rows(wout_ref, wout_buf, step)
        _convert_weight_rows(wgate_ref, wgate_buf, step)

    @pl.when(step == 0)
    def _():
        _convert_weight_rows(wp_ref, wp_buf, 0)

    @pl.when(step >= n_w)
    def _():
        tb = x_ref.shape[0]
        everything = slice(None)
        first = tb * OUT_FIRST_GROUP_EIGHTHS // 8
        halves = [slice(0, first), slice(first, tb)]
        ms = [jnp.dot(y_ref[h, :], _bf16_operand(wout_buf, everything),
                      preferred_element_type=F32) for h in halves]
        es = [jnp.dot(p_ref[h, :].astype(BF16), _bf16_operand(wp_buf, everything),
                      preferred_element_type=F32) for h in halves]
        rs = [alpha * x_ref[h, :] + m for h, m in zip(halves, ms)]
        gs = [jnp.dot(r.astype(BF16), _bf16_operand(wgate_buf, everything),
                      preferred_element_type=F32) for r in rs]
        for h, r, e, g in zip(halves, rs, es, gs):
            s = r + _sigmoid(g + bg_ref[this, :]) * e
            mu = jnp.mean(s, axis=-1, keepdims=True)
            d = s - mu
            var = jnp.mean(d * d, axis=-1, keepdims=True)
            o_ref[h, :] = d * lax.rsqrt(var + LN_EPS) * lng_ref[this, :] + lnb_ref[this, :]


def _resident(shape, layer):
    nd = len(shape)
    return pl.BlockSpec((None,) + tuple(shape), lambda *_: (layer,) + (0,) * nd,
                        pipeline_mode=pl.Buffered(1))


def _stacked(arr):
    return pl.BlockSpec(arr.shape, lambda *_: (0,) * arr.ndim, pipeline_mode=pl.Buffered(1))


def _weight_rows(rows, n_cols, layer):
    last = D_MODEL // rows - 1
    return pl.BlockSpec((None, rows, n_cols), lambda s: (layer, jnp.minimum(s, last), 0))


def _mixer(xp, xs, h0, lb, sb, params, layer, batch, seq, n_seq, n_t):
    tm = MIX_TILE
    p_rows = tm * PROMPT_TILES_PER_STEP
    steps_per_seq = seq // p_rows
    n_p = batch * steps_per_seq
    n_s = (n_seq * n_t) // tm
    n_w = D_MODEL // MIX_W_ROWS
    seq_per_tile = tm // n_t
    assert seq % p_rows == 0 and (n_seq * n_t) % tm == 0 and tm % n_t == 0

    prompt_step = lambda s: jnp.clip(s - n_w, 0, n_p - 1)
    sample_tile = lambda s: jnp.clip(s - n_w - n_p, 0, n_s - 1)
    once = pl.Buffered(1)
    p_tok = pl.BlockSpec((p_rows, D_MODEL), lambda s: (prompt_step(s), 0))
    s_tok = lambda **kw: pl.BlockSpec((tm, D_MODEL), lambda s: (sample_tile(s), 0), **kw)
    p_state = lambda k: pl.BlockSpec((None, k, D_LRU),
                                     lambda s: (prompt_step(s) // steps_per_seq, 0, 0))
    s_state_in = lambda k: pl.BlockSpec((None, k * seq_per_tile, D_LRU),
                                        lambda s: (layer, sample_tile(s), 0), pipeline_mode=once)
    s_state_out = lambda k: pl.BlockSpec((k * seq_per_tile, D_LRU), lambda s: (sample_tile(s), 0))

    (w_in, cw, cb, wa, wx, ba, bx, lam, sw, gnl, gns) = params
    return pl.pallas_call(
        functools.partial(_mixer_kernel, layer=layer, n_w=n_w, n_p=n_p,
                          steps_per_seq=steps_per_seq, n_t=n_t),
        grid=(n_w + n_p + n_s,),
        in_specs=[p_tok, s_tok(pipeline_mode=once), s_state_in(1), s_state_in(LRU_CONV_W - 1),
                  s_state_in(SC_CONV_W - 1),
                  _weight_rows(MIX_W_ROWS, N_PROJ * D_LRU, layer),
                  _resident((LRU_CONV_W, D_LRU), layer),
                  _stacked(cb),
                  _resident((LRU_HEADS, HEAD_DIM, HEAD_DIM), layer),
                  _resident((LRU_HEADS, HEAD_DIM, HEAD_DIM), layer),
                  _stacked(ba), _stacked(bx), _stacked(lam),
                  _resident((SC_CONV_W, D_SC), layer),
                  _stacked(gnl), _stacked(gns)],
        out_specs=[p_tok, s_tok(), p_state(1), p_state(LRU_CONV_W - 1), p_state(SC_CONV_W - 1),
                   s_state_out(1), s_state_out(LRU_CONV_W - 1), s_state_out(SC_CONV_W - 1)],
        out_shape=[jax.ShapeDtypeStruct((batch * seq, D_MODEL), BF16),
                   jax.ShapeDtypeStruct((n_seq * n_t, D_MODEL), BF16),
                   jax.ShapeDtypeStruct((batch, 1, D_LRU), F32),
                   jax.ShapeDtypeStruct((batch, LRU_CONV_W - 1, D_LRU), F32),
                   jax.ShapeDtypeStruct((batch, SC_CONV_W - 1, D_SC), F32),
                   jax.ShapeDtypeStruct((n_seq, D_LRU), F32),
                   jax.ShapeDtypeStruct(((LRU_CONV_W - 1) * n_seq, D_LRU), F32),
                   jax.ShapeDtypeStruct(((SC_CONV_W - 1) * n_seq, D_SC), F32)],
        scratch_shapes=[pltpu.VMEM((D_MODEL // 2, N_PROJ * D_LRU), jnp.uint32),
                        pltpu.VMEM((LRU_HEADS, HEAD_DIM, 2 * HEAD_DIM), BF16),
                        pltpu.VMEM((SUBLANES, D_LRU), F32),
                        pltpu.VMEM((SUBLANES, D_SC), F32),
                        pltpu.VMEM((1, D_LRU), F32),
                        pltpu.VMEM((PROMPT_TILES_PER_STEP, tm, D_MODEL), BF16)],
        compiler_params=pltpu.CompilerParams(
            dimension_semantics=("arbitrary",),
            vmem_limit_bytes=VMEM_LIMIT_BYTES),
        name=f"mixer_l{layer}",
    )(xp, xs, h0, lb, sb, w_in, cw, cb, wa, wx, ba, bx, lam, sw, gnl, gns)


def _output(x, y, p, params, layer, alpha, tag):
    n = x.shape[0]
    tb = min(OUT_TILE, n)
    n_w = D_MODEL // OUT_W_ROWS
    assert n % tb == 0
    tile_of = lambda s: jnp.maximum(s - n_w, 0)
    tile = lambda w: pl.BlockSpec((tb, w), lambda s: (tile_of(s), 0))
    (w_out, wp, wgate, bg, lng, lnb) = params
    return pl.pallas_call(
        functools.partial(_output_kernel, alpha=alpha, n_w=n_w, layer=layer),
        grid=(n_w + n // tb,),
        in_specs=[tile(D_MODEL), tile(D_MODEL),
                  pl.BlockSpec((None, tb, D_PLE), lambda s: (layer, tile_of(s), 0)),
                  _weight_rows(OUT_W_ROWS, D_MODEL, layer),
                  _resident((D_PLE, D_MODEL), layer),
                  _weight_rows(OUT_W_ROWS, D_MODEL, layer),
                  _stacked(bg), _stacked(lng), _stacked(lnb)],
        out_specs=tile(D_MODEL),
        out_shape=jax.ShapeDtypeStruct((n, D_MODEL), F32),
        scratch_shapes=[pltpu.VMEM((D_MODEL // 2, D_MODEL), jnp.uint32),
                        pltpu.VMEM((D_PLE // 2, D_MODEL), jnp.uint32),
                        pltpu.VMEM((D_MODEL // 2, D_MODEL), jnp.uint32)],
        compiler_params=pltpu.CompilerParams(
            dimension_semantics=("arbitrary",),
            vmem_limit_bytes=VMEM_LIMIT_BYTES),
        name=f"output_{tag}_l{layer}",
    )(x, y, p, w_out, wp, wgate, bg, lng, lnb)


def kernel(x_prompt, x_sample, state_lru_h, state_lru_conv, state_sc_conv, p_prompt, p_sample,
           w_in, lru_conv_w, lru_conv_b, lru_wa, lru_ba, lru_wx, lru_bx, lru_lambda,
           sc_conv_w, gn_lru, gn_sc, w_out, ple_wp, ple_wg, ple_bg, ln_g, ln_b):
    depth = w_in.shape[0]
    batch, seq, _ = x_prompt.shape
    n_seq, n_t, _ = x_sample.shape
    alpha = (2.0 * depth) ** 0.25
    seq_per_tile = MIX_TILE // n_t
    n_s = n_seq // seq_per_tile

    mixer_params = (w_in, lru_conv_w, lru_conv_b, lru_wa, lru_wx, lru_ba, lru_bx, lru_lambda,
                    sc_conv_w, gn_lru, gn_sc)
    out_params = (w_out, ple_wp, ple_wg, ple_bg, ln_g, ln_b)

    def to_slabs(v):
        *lead, n, k, c = v.shape
        v = jnp.swapaxes(v.reshape(*lead, n_s, seq_per_tile, k, c), -3, -2)
        return v.reshape(*lead, n * k, c)

    def from_slabs(v, k):
        c = v.shape[-1]
        v = jnp.swapaxes(v.reshape(n_s, k, seq_per_tile, c), 1, 2)
        return v.reshape(n_seq, k, c)

    xs = to_slabs(x_sample)
    ps = to_slabs(p_sample)
    lb = to_slabs(state_lru_conv)
    sb = to_slabs(state_sc_conv)

    xp = x_prompt.reshape(batch * seq, D_MODEL)
    pp = p_prompt.reshape(depth, batch * seq, D_PLE)

    hp, lcp, scp, hsm, lcs, scs = [], [], [], [], [], []
    for l in range(depth):
        yp, ys, h_p, lc_p, sc_p, h_s, lc_s, sc_s = _mixer(
            xp, xs, state_lru_h, lb, sb, mixer_params, l, batch, seq, n_seq, n_t)
        xp = _output(xp, yp, pp, out_params, l, alpha, "prompt")
        xs = _output(xs, ys, ps, out_params, l, alpha, "sample")
        hp.append(h_p.reshape(batch, D_LRU)); lcp.append(lc_p); scp.append(sc_p)
        hsm.append(h_s)
        lcs.append(from_slabs(lc_s, LRU_CONV_W - 1))
        scs.append(from_slabs(sc_s, SC_CONV_W - 1))

    y_prompt = xp.reshape(batch, seq, D_MODEL)
    y_sample = from_slabs(xs, n_t)
    return (y_prompt, y_sample, jnp.stack(hp), jnp.stack(lcp), jnp.stack(scp),
            jnp.stack(hsm), jnp.stack(lcs), jnp.stack(scs))
```
